```python
import math
import jax, jax.numpy as jnp
from jax import lax
import numpy as np


D_MODEL = 4096
BATCH = 2
SEQ = 8192
DEPTH = 2

N_A_LAYERS = DEPTH // 2
N_B_LAYERS = DEPTH - N_A_LAYERS

CHUNK = 128
D_GM = D_MODEL
GM_GROUPS = 8
GM_GROUP_W = D_GM // GM_GROUPS

HEAD_DIM = 128
N_HEADS = D_MODEL // (2 * HEAD_DIM)
Q_BLOCK = 128

N_BUCKETS = 32
MAX_DISTANCE = 128

N_GROUPS = 4
N_EXPERTS_PER_GROUP = 8
TOP_K = 2
D_EXPERT = D_MODEL // 8

EPS = 1e-6
NEG_INF = -1e30

kernel_name = 'yoco_gmlp_diffattn_hmoe'


def rms_norm(x, g, eps=EPS):
    xf = x.astype(jnp.float32)
    y = xf * lax.rsqrt(jnp.mean(xf * xf, axis=-1, keepdims=True) + eps)
    return (y * g.astype(jnp.float32)).astype(x.dtype)


def layer_norm(x, g, b, eps=1e-5):
    xf = x.astype(jnp.float32)
    mu = jnp.mean(xf, axis=-1, keepdims=True)
    var = jnp.mean(jnp.square(xf - mu), axis=-1, keepdims=True)
    y = (xf - mu) * lax.rsqrt(var + eps) * g.astype(jnp.float32) + b.astype(jnp.float32)
    return y.astype(x.dtype)


def lambda_init_fn(layer):
    return 0.8 - 0.6 * math.exp(-0.3 * layer)


def t5_causal_bucket(rel):
    n = jnp.maximum(rel, 0)
    max_exact = N_BUCKETS // 2
    nf = jnp.maximum(n, 1).astype(jnp.float32)
    large = max_exact + (jnp.log(nf / max_exact) / math.log(MAX_DISTANCE / max_exact)
                         * (N_BUCKETS - max_exact)).astype(jnp.int32)
    large = jnp.minimum(large, N_BUCKETS - 1)
    return jnp.where(n < max_exact, n, large)


def chunked_gmlp(x, norm_g, w_in, b_in, ln_g, ln_b, w_s, b_s, w_out):
    B, S, _ = x.shape
    h = rms_norm(x, norm_g)
    z = jax.nn.gelu(h @ w_in + b_in)
    u, v = jnp.split(z, 2, axis=-1)
    v = layer_norm(v, ln_g, ln_b)
    v = v.reshape(B, S // CHUNK, CHUNK, GM_GROUPS, GM_GROUP_W)
    causal = jnp.tril(jnp.ones((CHUNK, CHUNK), dtype=bool))
    w_s_c = jnp.where(causal[None], w_s, 0)
    s = jnp.einsum('gts,bnsgc->bntgc', w_s_c.astype(v.dtype), v)
    s = s + b_s.T[None, None, :, :, None].astype(v.dtype)
    s = s.reshape(B, S, D_GM)
    return (u * s) @ w_out


def shared_kv(h, kv_norm, w_k, w_v, k_norm):
    B, S, _ = h.shape
    hn = rms_norm(h, kv_norm)
    k = rms_norm((hn @ w_k).reshape(B, S, N_HEADS, 2, HEAD_DIM), k_norm)
    v = (hn @ w_v).reshape(B, S, N_HEADS, 2 * HEAD_DIM)
    return k, v


def diff_attention(x, k, v, rel_bias, norm_g, w_q, q_norm, lq1, lk1, lq2, lk2,
                   subln, w_o, lambda_init):
    B, S, _ = x.shape
    h = rms_norm(x, norm_g)
    q = rms_norm((h @ w_q).reshape(B, S, N_HEADS, 2, HEAD_DIM), q_norm)
    f32 = jnp.float32
    lam = (jnp.exp(jnp.sum(lq1.astype(f32) * lk1.astype(f32)))
           - jnp.exp(jnp.sum(lq2.astype(f32) * lk2.astype(f32))) + lambda_init)
    scale = HEAD_DIM ** -0.5
    n_blocks = S // Q_BLOCK
    q_blocks = q.reshape(B, n_blocks, Q_BLOCK, N_HEADS, 2, HEAD_DIM).transpose(1, 0, 2, 3, 4, 5)
    k_pos = jnp.arange(S, dtype=jnp.int32)

    def block(args):
        qb, i = args
        q_pos = i * Q_BLOCK + jnp.arange(Q_BLOCK, dtype=jnp.int32)
        rel = q_pos[:, None] - k_pos[None, :]
        bias = jnp.transpose(rel_bias[t5_causal_bucket(rel)], (2, 0, 1)).astype(f32)
        logits = jnp.einsum('bqhcd,bkhcd->bhcqk', qb, k).astype(f32) * scale
        logits = logits + bias[None, :, None]
        logits = jnp.where(rel >= 0, logits, NEG_INF)
        p = jax.nn.softmax(logits, axis=-1)
        attn = p[:, :, 0] - lam * p[:, :, 1]
        return jnp.einsum('bhqk,bkhe->bqhe', attn.astype(v.dtype), v)

    o = lax.map(block, (q_blocks, jnp.arange(n_blocks, dtype=jnp.int32)))
    o = o.transpose(1, 0, 2, 3, 4).reshape(B, S, N_HEADS, 2 * HEAD_DIM)
    o = rms_norm(o, subln) * (1.0 - lambda_init)
    return o.reshape(B, S, N_HEADS * 2 * HEAD_DIM) @ w_o


def hier_moe(x, norm_g, w_group, w_router, w1, w3, w2):
    B, S, D = x.shape
    f32 = jnp.float32
    h = rms_norm(x, norm_g).reshape(B * S, D)
    g_prob = jax.nn.softmax((h @ w_group).astype(f32), axis=-1)
    g_idx = jnp.argmax(g_prob, axis=-1)
    g_gate = jnp.take_along_axis(g_prob, g_idx[:, None], axis=-1)[:, 0]
    e_logits = jnp.einsum('td,gde->tge', h, w_router).astype(f32)
    e_logits = jnp.take_along_axis(e_logits, g_idx[:, None, None], axis=1)[:, 0]
    top_val, top_idx = lax.top_k(e_logits, TOP_K)
    top_w = jax.nn.softmax(top_val, axis=-1) * g_gate[:, None]
    e_w = jnp.sum(jax.nn.one_hot(top_idx, N_EXPERTS_PER_GROUP, dtype=f32) * top_w[..., None], axis=1)
    combine = (jax.nn.one_hot(g_idx, N_GROUPS, dtype=f32)[:, :, None] * e_w[:, None, :]).astype(h.dtype)
    y = jnp.zeros_like(h)
    for g in range(N_GROUPS):
        a = jnp.einsum('td,edf->tef', h, w1[g])
        b = jnp.einsum('td,edf->tef', h, w3[g])
        hid = jax.nn.silu(a) * b * combine[:, g, :, None]
        y = y + jnp.einsum('tef,efd->td', hid, w2[g])
    return y.reshape(B, S, D)


def setup_inputs(seed: int = 0) -> dict:
    key = jax.random.key(seed)
    ks = iter(jax.random.split(key, 40))
    nrm = lambda shape, s: jax.random.normal(next(ks), shape, jnp.float32) * s
    gain = lambda shape: 1.0 + nrm(shape, 0.02)
    D, H, dh = D_MODEL, N_HEADS, HEAD_DIM
    G, E, F = N_GROUPS, N_EXPERTS_PER_GROUP, D_EXPERT
    nA, nB = N_A_LAYERS, N_B_LAYERS
    return {
        'x': nrm((BATCH, SEQ, D), 1.0),
        'a_norm': gain((nA, D)),
        'a_w_in': nrm((nA, D, 2 * D_GM), D ** -0.5),
        'a_b_in': nrm((nA, 2 * D_GM), 0.02),
        'a_ln_g': gain((nA, D_GM)),
        'a_ln_b': nrm((nA, D_GM), 0.02),
        'a_w_s': nrm((nA, GM_GROUPS, CHUNK, CHUNK), CHUNK ** -0.5),
        'a_b_s': gain((nA, GM_GROUPS, CHUNK)),
        'a_w_out': nrm((nA, D_GM, D), D_GM ** -0.5),
        'kv_norm': gain((D,)),
        'w_k': nrm((D, H * 2 * dh), D ** -0.5),
        'w_v': nrm((D, H * 2 * dh), D ** -0.5),
        'k_norm': gain((dh,)),
        'b_norm': gain((nB, D)),
        'w_q': nrm((nB, D, H * 2 * dh), D ** -0.5),
        'q_norm': gain((nB, dh)),
        'lam_q1': nrm((nB, dh), 0.1),
        'lam_k1': nrm((nB, dh), 0.1),
        'lam_q2': nrm((nB, dh), 0.1),
        'lam_k2': nrm((nB, dh), 0.1),
        'subln': gain((nB, 2 * dh)),
        'w_o': nrm((nB, H * 2 * dh, D), (H * 2 * dh) ** -0.5),
        'rel_bias': nrm((N_BUCKETS, H), 0.5),
        'm_norm': gain((DEPTH, D)),
        'm_w_group': nrm((DEPTH, D, G), D ** -0.5),
        'm_w_router': nrm((DEPTH, G, D, E), D ** -0.5),
        'm_w1': nrm((DEPTH, G, E, D, F), D ** -0.5),
        'm_w3': nrm((DEPTH, G, E, D, F), D ** -0.5),
        'm_w2': nrm((DEPTH, G, E, F, D), F ** -0.5),
    }


def reference(x, a_norm, a_w_in, a_b_in, a_ln_g, a_ln_b, a_w_s, a_b_s, a_w_out,
              kv_norm, w_k, w_v, k_norm,
              b_norm, w_q, q_norm, lam_q1, lam_k1, lam_q2, lam_k2, subln, w_o,
              rel_bias,
              m_norm, m_w_group, m_w_router, m_w1, m_w3, m_w2):
    h = x
    k = None
    v = None
    for layer in range(DEPTH):
        if layer < N_A_LAYERS:
            i = layer
            h = h + chunked_gmlp(h, a_norm[i], a_w_in[i], a_b_in[i], a_ln_g[i], a_ln_b[i],
                                 a_w_s[i], a_b_s[i], a_w_out[i])
        else:
            if layer == N_A_LAYERS:
                k, v = shared_kv(h, kv_norm, w_k, w_v, k_norm)
            j = layer - N_A_LAYERS
            h = h + diff_attention(h, k, v, rel_bias, b_norm[j], w_q[j], q_norm[j],
                                   lam_q1[j], lam_k1[j], lam_q2[j], lam_k2[j],
                                   subln[j], w_o[j], lambda_init_fn(layer))
        h = h + hier_moe(h, m_norm[layer], m_w_group[layer], m_w_router[layer],
                         m_w1[layer], m_w3[layer], m_w2[layer])
    return h
```

```python
import functools
import math

import jax
import jax.numpy as jnp
import numpy as np
from jax import lax
from jax.experimental import pallas as pl
from jax.experimental.pallas import tpu as pltpu

F32 = jnp.float32
BF16 = jnp.bfloat16

CHUNK = 128
HEAD_DIM = 128
N_BUCKETS = 32
MAX_DISTANCE = 128
TOP_K = 2
EPS = 1e-6
LN_EPS = 1e-5
NEG_INF = -1e30

LANES = 128
VMEM_LIMIT_BYTES = 56 * 1024 * 1024

NORM_TM = 512
MM_TM, MM_TN = 512, 1024
GATE_TM = 512
ATTN_TQ = 512
ROUTER_TM = 512
MOE_TM = 256
COMBINE_TM = 128


def _params(semantics):
    return pltpu.CompilerParams(dimension_semantics=semantics, vmem_limit_bytes=VMEM_LIMIT_BYTES)


def _lambda_init(layer):
    return 0.8 - 0.6 * math.exp(-0.3 * layer)


def _rmsnorm_kernel(x_ref, g_ref, *o_refs):
    x = x_ref[...]
    y = x * lax.rsqrt(jnp.mean(x * x, axis=-1, keepdims=True) + EPS)
    for k, o_ref in enumerate(o_refs):
        o_ref[...] = (y * g_ref[k:k + 1, :]).astype(o_ref.dtype)


def _rmsnorm(x, gains):
    t, d = x.shape
    n = gains.shape[0]
    tm = min(NORM_TM, t)
    return pl.pallas_call(
        _rmsnorm_kernel,
        grid=(t // tm,),
        in_specs=[pl.BlockSpec((tm, d), lambda i: (i, 0)),
                  pl.BlockSpec((n, d), lambda i: (0, 0))],
        out_specs=[pl.BlockSpec((tm, d), lambda i: (i, 0))] * n,
        out_shape=[jax.ShapeDtypeStruct((t, d), BF16)] * n,
        compiler_params=_params(("parallel",)),
    )(x, gains)


def _gelu_tanh(x):
    return 0.5 * x * (1.0 + jnp.tanh(math.sqrt(2.0 / math.pi) * (x + 0.044715 * (x * x * x))))


def _mm_kernel(a_ref, w_ref, *rest, epilogue, scale):
    o_ref = rest[-1]
    acc = jnp.dot(a_ref[...], w_ref[...], preferred_element_type=F32)
    if epilogue == "gelu_bias":
        o_ref[...] = _gelu_tanh(acc + rest[0][...]).astype(o_ref.dtype)
    elif epilogue == "residual":
        o_ref[...] = rest[0][...] + acc
    elif epilogue == "headnorm":
        gain = rest[0][...] * scale
        for g in range(acc.shape[1] // HEAD_DIM):
            y = acc[:, g * HEAD_DIM:(g + 1) * HEAD_DIM]
            y = y * lax.rsqrt(jnp.mean(y * y, axis=-1, keepdims=True) + EPS)
            o_ref[:, g * HEAD_DIM:(g + 1) * HEAD_DIM] = (y * gain).astype(o_ref.dtype)
    else:
        o_ref[...] = acc.astype(o_ref.dtype)


def _matmul(a, w, *, epilogue="none", extra=None, scale=1.0, out_dtype=BF16):
    m, k = a.shape
    n = w.shape[1]
    tm, tn = min(MM_TM, m), min(MM_TN, n)
    in_specs = [pl.BlockSpec((tm, k), lambda j, i: (i, 0)),
                pl.BlockSpec((k, tn), lambda j, i: (0, j))]
    args = [a, w]
    if epilogue == "gelu_bias":
        in_specs.append(pl.BlockSpec((1, tn), lambda j, i: (0, j)))
        args.append(extra.reshape(1, n))
    elif epilogue == "residual":
        in_specs.append(pl.BlockSpec((tm, tn), lambda j, i: (i, j)))
        args.append(extra)
    elif epilogue == "headnorm":
        in_specs.append(pl.BlockSpec((1, HEAD_DIM), lambda j, i: (0, 0)))
        args.append(extra.reshape(1, HEAD_DIM))
    return pl.pallas_call(
        functools.partial(_mm_kernel, epilogue=epilogue, scale=scale),
        grid=(n // tn, m // tm),
        in_specs=in_specs,
        out_specs=pl.BlockSpec((tm, tn), lambda j, i: (i, j)),
        out_shape=jax.ShapeDtypeStruct((m, n), out_dtype),
        compiler_params=_params(("parallel", "parallel")),
    )(*args)


def _gmlp_gate_kernel(u_ref, v_ref, lng_ref, lnb_ref, ws_ref, bs_ref, o_ref, *, n_chunks, groups):
    v = v_ref[...].astype(F32)
    mu = jnp.mean(v, axis=-1, keepdims=True)
    vc = v - mu
    var = jnp.mean(vc * vc, axis=-1, keepdims=True)
    vn = (vc * lax.rsqrt(var + LN_EPS) * lng_ref[...] + lnb_ref[...]).astype(BF16)
    gw = v.shape[1] // groups
    row = lax.broadcasted_iota(jnp.int32, (CHUNK, CHUNK), 0)
    col = lax.broadcasted_iota(jnp.int32, (CHUNK, CHUNK), 1)
    causal = row >= col
    for g in range(groups):
        w = jnp.where(causal, ws_ref[g], 0.0).astype(BF16)
        b = bs_ref[g]
        for c in range(n_chunks):
            rows = slice(c * CHUNK, (c + 1) * CHUNK)
            cols = slice(g * gw, (g + 1) * gw)
            s = jnp.dot(w, vn[rows, cols], preferred_element_type=F32) + b
            o_ref[rows, cols] = (u_ref[rows, cols].astype(F32) * s).astype(o_ref.dtype)


def _gmlp_gate(z, ln_g, ln_b, w_s, b_s):
    t, d2 = z.shape
    dg = d2 // 2
    groups = w_s.shape[0]
    tm = min(GATE_TM, t)
    return pl.pallas_call(
        functools.partial(_gmlp_gate_kernel, n_chunks=tm // CHUNK, groups=groups),
        grid=(t // tm,),
        in_specs=[pl.BlockSpec((tm, dg), lambda i: (i, 0)),
                  pl.BlockSpec((tm, dg), lambda i: (i, 1)),
                  pl.BlockSpec((1, dg), lambda i: (0, 0)),
                  pl.BlockSpec((1, dg), lambda i: (0, 0)),
                  pl.BlockSpec((groups, CHUNK, CHUNK), lambda i: (0, 0, 0)),
                  pl.BlockSpec((groups, CHUNK, 1), lambda i: (0, 0, 0))],
        out_specs=pl.BlockSpec((tm, dg), lambda i: (i, 0)),
        out_shape=jax.ShapeDtypeStruct((t, dg), BF16),
        compiler_params=_params(("parallel",)),
    )(z, z, ln_g.reshape(1, dg), ln_b.reshape(1, dg), w_s, b_s.reshape(groups, CHUNK, 1))


def _t5_causal_bucket(rel):
    n = jnp.maximum(rel, 0)
    max_exact = N_BUCKETS // 2
    nf = jnp.maximum(n, 1).astype(F32)
    large = max_exact + (jnp.log(nf / max_exact) / math.log(MAX_DISTANCE / max_exact)
                         * (N_BUCKETS - max_exact)).astype(jnp.int32)
    large = jnp.minimum(large, N_BUCKETS - 1)
    return jnp.where(n < max_exact, n, large)


def _attn_kernel(lam_ref, q_ref, k_ref, v_ref, bias_ref, subln_ref, o_ref, acc_ref, m_ref, l_ref,
                 *, tq, lambda_init):
    qi = pl.program_id(2)
    m_ref[...] = jnp.full(m_ref.shape, NEG_INF, F32)
    l_ref[...] = jnp.zeros(l_ref.shape, F32)
    acc_ref[...] = jnp.zeros(acc_ref.shape, F32)
    row = lax.broadcasted_iota(jnp.int32, (tq, tq), 0)
    col = lax.broadcasted_iota(jnp.int32, (tq, tq), 1)
    causal = row >= col

    def step(j, bias, masked):
        ks = pl.multiple_of(j * tq, tq)
        vblk = v_ref[0, pl.ds(ks, tq), :]
        for c in range(2):
            q = q_ref[:, c * HEAD_DIM:(c + 1) * HEAD_DIM]
            k = k_ref[0, pl.ds(ks, tq), c * HEAD_DIM:(c + 1) * HEAD_DIM]
            s = lax.dot_general(q, k, (((1,), (1,)), ((), ())), preferred_element_type=F32)
            if bias is not None:
                s = s + bias
            if masked:
                s = jnp.where(causal, s, NEG_INF)
            m_prev = m_ref[c]
            m_new = jnp.maximum(m_prev, jnp.max(s, axis=-1, keepdims=True))
            alpha = jnp.exp(m_prev - m_new)
            p = jnp.exp(s - m_new)
            l_ref[c] = alpha * l_ref[c] + jnp.sum(p, axis=-1, keepdims=True)
            acc_ref[c] = alpha * acc_ref[c] + jnp.dot(p.astype(BF16), vblk, preferred_element_type=F32)
            m_ref[c] = m_new

    def far_body(j, carry):
        step(j, None, False)
        return carry

    lax.fori_loop(0, jnp.maximum(qi - 1, 0), far_body, 0)

    @pl.when(qi >= 1)
    def _():
        step(qi - 1, bias_ref[0, 1], False)

    step(qi, bias_ref[0, 0], True)

    lam_v = lam_ref[...]
    lam = (jnp.exp(jnp.sum(lam_v[0:1] * lam_v[1:2], axis=-1, keepdims=True))
           - jnp.exp(jnp.sum(lam_v[2:3] * lam_v[3:4], axis=-1, keepdims=True)) + lambda_init)
    o = acc_ref[0] / l_ref[0] - lam * (acc_ref[1] / l_ref[1])
    o = o * lax.rsqrt(jnp.mean(o * o, axis=-1, keepdims=True) + EPS)
    o_ref[...] = (o * subln_ref[...] * (1.0 - lambda_init)).astype(o_ref.dtype)


def _diff_attention(q, k, v, rel_bias, lam_vecs, subln, *, batch, seq, lambda_init):
    t, d = q.shape
    hd2 = 2 * HEAD_DIM
    n_heads = d // hd2
    tq = min(ATTN_TQ, seq)
    nq = seq // tq
    assert tq >= CHUNK and seq % tq == 0
    far = np.arange(tq + 1, max(seq, tq + 2), dtype=np.int64)
    far_bucket = N_BUCKETS // 2 + (np.log(far.astype(np.float32) / (N_BUCKETS // 2))
                                   / math.log(MAX_DISTANCE / (N_BUCKETS // 2))
                                   * (N_BUCKETS - N_BUCKETS // 2)).astype(np.int64)
    assert np.all(far_bucket >= N_BUCKETS - 1)
    table = (rel_bias - rel_bias[N_BUCKETS - 1:N_BUCKETS, :]).astype(F32)
    r = jnp.arange(tq, dtype=jnp.int32)
    rel0 = r[:, None] - r[None, :]
    rel = jnp.stack([rel0, rel0 + tq])
    bias = jnp.transpose(table[_t5_causal_bucket(rel)], (3, 0, 1, 2))
    k3 = k.reshape(batch, seq, d)
    v3 = v.reshape(batch, seq, d)
    return pl.pallas_call(
        functools.partial(_attn_kernel, tq=tq, lambda_init=lambda_init),
        grid=(batch, n_heads, nq),
        in_specs=[pl.BlockSpec((4, HEAD_DIM), lambda b, h, i: (0, 0)),
                  pl.BlockSpec((tq, hd2), lambda b, h, i: (b * nq + i, h)),
                  pl.BlockSpec((1, seq, hd2), lambda b, h, i: (b, 0, h)),
                  pl.BlockSpec((1, seq, hd2), lambda b, h, i: (b, 0, h)),
                  pl.BlockSpec((1, 2, tq, tq), lambda b, h, i: (h, 0, 0, 0)),
                  pl.BlockSpec((1, hd2), lambda b, h, i: (0, 0))],
        out_specs=pl.BlockSpec((tq, hd2), lambda b, h, i: (b * nq + i, h)),
        out_shape=jax.ShapeDtypeStruct((t, d), BF16),
        scratch_shapes=[pltpu.VMEM((2, tq, hd2), F32),
                        pltpu.VMEM((2, tq, 1), F32),
                        pltpu.VMEM((2, tq, 1), F32)],
        compiler_params=_params(("parallel", "parallel", "arbitrary")),
    )(lam_vecs, q, k3, v3, bias, subln.reshape(1, hd2))


def _router_kernel(x_ref, g_ref, wr_ref, o_ref, *, n_groups, n_experts):
    x = x_ref[...]
    hn = x * lax.rsqrt(jnp.mean(x * x, axis=-1, keepdims=True) + EPS) * g_ref[...]
    logits = jnp.dot(hn, wr_ref[...], preferred_element_type=F32, precision=lax.Precision.HIGHEST)
    lane = lax.broadcasted_iota(jnp.int32, logits.shape, 1).astype(F32)
    no_lane = float(LANES)
    is_g = lane < n_groups
    gl = jnp.where(is_g, logits, -jnp.inf)
    gmax = jnp.max(gl, axis=-1, keepdims=True)
    gidx = jnp.min(jnp.where(gl == gmax, lane, no_lane), axis=-1, keepdims=True)
    gsum = jnp.sum(jnp.where(is_g, jnp.exp(gl - gmax), 0.0), axis=-1, keepdims=True)
    g_gate = 1.0 / gsum
    lo = n_groups + gidx * n_experts
    in_grp = (lane >= lo) & (lane < lo + n_experts)
    el = jnp.where(in_grp, logits, -jnp.inf)
    t1 = jnp.max(el, axis=-1, keepdims=True)
    i1 = jnp.min(jnp.where(el == t1, lane, no_lane), axis=-1, keepdims=True)
    el2 = jnp.where(lane == i1, -jnp.inf, el)
    t2 = jnp.max(el2, axis=-1, keepdims=True)
    i2 = jnp.min(jnp.where(el2 == t2, lane, no_lane), axis=-1, keepdims=True)
    e = jnp.exp(t2 - t1)
    w1 = g_gate / (1.0 + e)
    w2 = g_gate * e / (1.0 + e)
    out = jnp.where(lane == 0, i1 - n_groups,
                    jnp.where(lane == 1, i2 - n_groups,
                              jnp.where(lane == 2, w1, jnp.where(lane == 3, w2, 0.0))))
    o_ref[...] = out


def _router(h, gain, w_group, w_router):
    t, d = h.shape
    n_groups, _, n_experts = w_router.shape
    tm = min(ROUTER_TM, t)
    assert n_groups + n_groups * n_experts <= LANES
    wr = jnp.concatenate([w_group, jnp.transpose(w_router, (1, 0, 2)).reshape(d, n_groups * n_experts)], axis=1)
    wr = jnp.pad(wr, ((0, 0), (0, LANES - wr.shape[1])))
    return pl.pallas_call(
        functools.partial(_router_kernel, n_groups=n_groups, n_experts=n_experts),
        grid=(t // tm,),
        in_specs=[pl.BlockSpec((tm, d), lambda i: (i, 0)),
                  pl.BlockSpec((1, d), lambda i: (0, 0)),
                  pl.BlockSpec((d, LANES), lambda i: (0, 0))],
        out_specs=pl.BlockSpec((tm, LANES), lambda i: (i, 0)),
        out_shape=jax.ShapeDtypeStruct((t, LANES), F32),
        compiler_params=_params(("parallel",)),
    )(h, gain.reshape(1, d), wr)


def _row_copy(src_hbm, idx, buf, slot, r, sem):
    return pltpu.make_async_copy(src_hbm.at[pl.ds(idx, 1), :], buf.at[slot, pl.ds(r, 1), :], sem.at[slot])


def _gather_start(src_hbm, idx_ref, base, buf, slot, sem, n_rows):
    def body(r, c):
        _row_copy(src_hbm, idx_ref[base + r], buf, slot, r, sem).start()
        return c
    lax.fori_loop(0, n_rows, body, 0, unroll=8)


def _gather_wait(src_hbm, buf, slot, sem, n_rows):
    def body(r, c):
        _row_copy(src_hbm, 0, buf, slot, r, sem).wait()
        return c
    lax.fori_loop(0, n_rows, body, 0, unroll=8)


def _moe_up_kernel(te_ref, nv_ref, tok_ref, h_hbm, g_ref, w1_ref, w3_ref, gate_ref, o_ref, buf, sem, *, tm):
    t = pl.program_id(0)
    n_valid = nv_ref[0]
    slot = t % 2

    @pl.when(t == 0)
    def _():
        _gather_start(h_hbm, tok_ref, 0, buf, 0, sem, tm)

    @pl.when(t + 1 < n_valid)
    def _():
        _gather_start(h_hbm, tok_ref, (t + 1) * tm, buf, 1 - slot, sem, tm)

    @pl.when((t < n_valid) | (t == 0))
    def _():
        _gather_wait(h_hbm, buf, slot, sem, tm)

    @pl.when(t < n_valid)
    def _():
        x = buf[slot]
        hn = (x * lax.rsqrt(jnp.mean(x * x, axis=-1, keepdims=True) + EPS) * g_ref[...]).astype(BF16)
        a = jnp.dot(hn, w1_ref[0], preferred_element_type=F32)
        b = jnp.dot(hn, w3_ref[0], preferred_element_type=F32)
        o_ref[...] = (a * jax.nn.sigmoid(a) * b * gate_ref[...]).astype(o_ref.dtype)

    @pl.when(t >= n_valid)
    def _():
        o_ref[...] = jnp.zeros(o_ref.shape, o_ref.dtype)


def _moe_down_kernel(te_ref, nv_ref, h_ref, w2_ref, o_ref):
    t = pl.program_id(0)

    @pl.when(t < nv_ref[0])
    def _():
        o_ref[...] = jnp.dot(h_ref[...], w2_ref[0], preferred_element_type=F32)

    @pl.when(t >= nv_ref[0])
    def _():
        o_ref[...] = jnp.zeros(o_ref.shape, o_ref.dtype)


def _combine_kernel(pos_ref, y_hbm, h_ref, o_ref, buf0, buf1, sem0, sem1, *, tm, n_tok):
    t = pl.program_id(0)
    n = pl.num_programs(0)
    slot = t % 2

    def start(step, s):
        _gather_start(y_hbm, pos_ref, step * tm, buf0, s, sem0, tm)
        _gather_start(y_hbm, pos_ref, n_tok + step * tm, buf1, s, sem1, tm)

    @pl.when(t == 0)
    def _():
        start(0, 0)

    @pl.when(t + 1 < n)
    def _():
        start(t + 1, 1 - slot)

    _gather_wait(y_hbm, buf0, slot, sem0, tm)
    _gather_wait(y_hbm, buf1, slot, sem1, tm)
    o_ref[...] = h_ref[...] + buf0[slot] + buf1[slot]


def _hier_moe(h, gain, w_group, w_router, w1, w3, w2):
    t, d = h.shape
    ne, _, f = w1.shape
    tm, tc = MOE_TM, COMBINE_TM
    route = _router(h, gain, w_group, w_router)
    e_pair = jnp.concatenate([route[:, 0], route[:, 1]]).astype(jnp.int32)
    w_pair = jnp.concatenate([route[:, 2], route[:, 3]])
    tok_pair = jnp.tile(jnp.arange(t, dtype=jnp.int32), TOP_K)

    onehot = (e_pair[:, None] == jnp.arange(ne, dtype=jnp.int32)[None, :]).astype(jnp.int32)
    rank = jnp.take_along_axis(jnp.cumsum(onehot, axis=0) - onehot, e_pair[:, None], axis=1)[:, 0]
    counts = jnp.sum(onehot, axis=0)
    tiles_e = (counts + tm - 1) // tm
    tile_end = jnp.cumsum(tiles_e)
    dest = (tile_end - tiles_e)[e_pair] * tm + rank
    n_tiles = (TOP_K * t) // tm + ne
    n_rows = n_tiles * tm
    tok_sorted = jnp.zeros((n_rows,), jnp.int32).at[dest].set(tok_pair)
    gate_sorted = jnp.zeros((n_rows,), F32).at[dest].set(w_pair).reshape(n_rows, 1)
    n_valid = tile_end[-1]
    tile_ids = jnp.minimum(jnp.arange(n_tiles, dtype=jnp.int32), n_valid - 1)
    tile_expert = jnp.searchsorted(tile_end, tile_ids, side="right").astype(jnp.int32)
    n_valid = n_valid.reshape(1).astype(jnp.int32)

    hid = pl.pallas_call(
        functools.partial(_moe_up_kernel, tm=tm),
        grid_spec=pltpu.PrefetchScalarGridSpec(
            num_scalar_prefetch=3,
            grid=(n_tiles,),
            in_specs=[pl.BlockSpec(memory_space=pl.ANY),
                      pl.BlockSpec((1, d), lambda i, te, nv, tok: (0, 0)),
                      pl.BlockSpec((1, d, f), lambda i, te, nv, tok: (te[i], 0, 0)),
                      pl.BlockSpec((1, d, f), lambda i, te, nv, tok: (te[i], 0, 0)),
                      pl.BlockSpec((tm, 1), lambda i, te, nv, tok: (i, 0))],
            out_specs=pl.BlockSpec((tm, f), lambda i, te, nv, tok: (i, 0)),
            scratch_shapes=[pltpu.VMEM((2, tm, d), F32), pltpu.SemaphoreType.DMA((2,))]),
        out_shape=jax.ShapeDtypeStruct((n_rows, f), BF16),
        compiler_params=_params(("arbitrary",)),
    )(tile_expert, n_valid, tok_sorted, h, gain.reshape(1, d), w1, w3, gate_sorted)

    ys = pl.pallas_call(
        _moe_down_kernel,
        grid_spec=pltpu.PrefetchScalarGridSpec(
            num_scalar_prefetch=2,
            grid=(n_tiles,),
            in_specs=[pl.BlockSpec((tm, f), lambda i, te, nv: (i, 0)),
                      pl.BlockSpec((1, f, d), lambda i, te, nv: (te[i], 0, 0))],
            out_specs=pl.BlockSpec((tm, d), lambda i, te, nv: (i, 0))),
        out_shape=jax.ShapeDtypeStruct((n_rows, d), F32),
        compiler_params=_params(("arbitrary",)),
    )(tile_expert, n_valid, hid, w2)

    return pl.pallas_call(
        functools.partial(_combine_kernel, tm=tc, n_tok=t),
        grid_spec=pltpu.PrefetchScalarGridSpec(
            num_scalar_prefetch=1,
            grid=(t // tc,),
            in_specs=[pl.BlockSpec(memory_space=pl.ANY),
                      pl.BlockSpec((tc, d), lambda i, pos: (i, 0))],
            out_specs=pl.BlockSpec((tc, d), lambda i, pos: (i, 0)),
            scratch_shapes=[pltpu.VMEM((2, tc, d), F32), pltpu.VMEM((2, tc, d), F32),
                            pltpu.SemaphoreType.DMA((2,)), pltpu.SemaphoreType.DMA((2,))]),
        out_shape=jax.ShapeDtypeStruct((t, d), F32),
        compiler_params=_params(("arbitrary",)),
    )(dest.astype(jnp.int32), ys, h)


def kernel(x, a_norm, a_w_in, a_b_in, a_ln_g, a_ln_b, a_w_s, a_b_s, a_w_out, kv_norm, w_k, w_v, k_norm, b_norm, w_q, q_norm, lam_q1, lam_k1, lam_q2, lam_k2, subln, w_o, rel_bias, m_norm, m_w_group, m_w_router, m_w1, m_w3, m_w2):
    batch, seq, d = x.shape
    depth = m_norm.shape[0]
    n_a = a_norm.shape[0]
    h = x.reshape(batch * seq, d)
    k = v = None
    for layer in range(depth):
        if layer < n_a:
            i = layer
            (hn,) = _rmsnorm(h, a_norm[i:i + 1])
            z = _matmul(hn, a_w_in[i].astype(BF16), epilogue="gelu_bias", extra=a_b_in[i])
            gz = _gmlp_gate(z, a_ln_g[i], a_ln_b[i], a_w_s[i], a_b_s[i])
            h = _matmul(gz, a_w_out[i].astype(BF16), epilogue="residual", extra=h, out_dtype=F32)
        else:
            j = layer - n_a
            if layer == n_a:
                hkv, hq = _rmsnorm(h, jnp.stack([kv_norm, b_norm[j]]))
                k = _matmul(hkv, w_k.astype(BF16), epilogue="headnorm", extra=k_norm)
                v = _matmul(hkv, w_v.astype(BF16))
            else:
                (hq,) = _rmsnorm(h, b_norm[j:j + 1])
            q = _matmul(hq, w_q[j].astype(BF16), epilogue="headnorm", extra=q_norm[j], scale=HEAD_DIM ** -0.5)
            lam_vecs = jnp.stack([lam_q1[j], lam_k1[j], lam_q2[j], lam_k2[j]]).astype(F32)
            o = _diff_attention(q, k, v, rel_bias, lam_vecs, subln[j], batch=batch, seq=seq,
                                lambda_init=_lambda_init(layer))
            h = _matmul(o, w_o[j].astype(BF16), epilogue="residual", extra=h, out_dtype=F32)
        n_g, n_e = m_w1.shape[1], m_w1.shape[2]
        f = m_w1.shape[-1]
        h = _hier_moe(h, m_norm[layer], m_w_group[layer], m_w_router[layer],
                      m_w1[layer].reshape(n_g * n_e, d, f).astype(BF16),
                      m_w3[layer].reshape(n_g * n_e, d, f).astype(BF16),
                      m_w2[layer].reshape(n_g * n_e, f, d).astype(BF16))
    return h.reshape(batch, seq, d)
```

```python
import functools
import math

import jax
import jax.numpy as jnp
import numpy as np
from jax import lax
from jax.experimental import pallas as pl
from jax.experimental.pallas import tpu as pltpu

F32 = jnp.float32
BF16 = jnp.bfloat16

CHUNK = 128
HEAD_DIM = 128
N_BUCKETS = 32
MAX_DISTANCE = 128
TOP_K = 2
EPS = 1e-6
LN_EPS = 1e-5
NEG_INF = -1e30

LANES = 128
VMEM_LIMIT_BYTES = 56 * 1024 * 1024

NORM_TM = 512
MM_TM, MM_TN = 512, 512
GATE_TM = 512
ATTN_TQ = 512
ROUTER_TM = 512
MOE_TM = 256
COMBINE_TM = 128


def _params(semantics):
    return pltpu.CompilerParams(dimension_semantics=semantics, vmem_limit_bytes=VMEM_LIMIT_BYTES)


def _lambda_init(layer):
    return 0.8 - 0.6 * math.exp(-0.3 * layer)


def _rmsnorm_kernel(x_ref, g_ref, *o_refs):
    x = x_ref[...]
    y = x * lax.rsqrt(jnp.mean(x * x, axis=-1, keepdims=True) + EPS)
    for k, o_ref in enumerate(o_refs):
        o_ref[...] = (y * g_ref[k:k + 1, :]).astype(o_ref.dtype)


def _rmsnorm(x, gains):
    t, d = x.shape
    n = gains.shape[0]
    tm = min(NORM_TM, t)
    return pl.pallas_call(
        _rmsnorm_kernel,
        grid=(t // tm,),
        in_specs=[pl.BlockSpec((tm, d), lambda i: (i, 0)),
                  pl.BlockSpec((n, d), lambda i: (0, 0))],
        out_specs=[pl.BlockSpec((tm, d), lambda i: (i, 0))] * n,
        out_shape=[jax.ShapeDtypeStruct((t, d), BF16)] * n,
        name="rmsnorm",
        compiler_params=_params(("parallel",)),
    )(x, gains)


def _gelu_tanh(x):
    return 0.5 * x * (1.0 + jnp.tanh(math.sqrt(2.0 / math.pi) * (x + 0.044715 * (x * x * x))))


def _mm_kernel(a_ref, w_ref, *rest, epilogue, scale):
    o_ref, wb_ref = rest[-2], rest[-1]

    @pl.when(pl.program_id(1) == 0)
    def _():
        wb_ref[...] = w_ref[...].astype(BF16)

    acc = jnp.dot(a_ref[...], wb_ref[...], preferred_element_type=F32)
    if epilogue == "gelu_bias":
        o_ref[...] = _gelu_tanh(acc + rest[0][...]).astype(o_ref.dtype)
    elif epilogue == "residual":
        o_ref[...] = rest[0][...] + acc
    elif epilogue == "headnorm":
        gain = rest[0][...] * scale
        for g in range(acc.shape[1] // HEAD_DIM):
            y = acc[:, g * HEAD_DIM:(g + 1) * HEAD_DIM]
            y = y * lax.rsqrt(jnp.mean(y * y, axis=-1, keepdims=True) + EPS)
            o_ref[:, g * HEAD_DIM:(g + 1) * HEAD_DIM] = (y * gain).astype(o_ref.dtype)
    else:
        o_ref[...] = acc.astype(o_ref.dtype)


def _matmul(a, w, layer=0, *, epilogue="none", extra=None, scale=1.0, out_dtype=BF16):
    m, k = a.shape
    if w.ndim == 2:
        w = w.reshape(1, *w.shape)
    n = w.shape[2]
    tm, tn = min(MM_TM, m), min(MM_TN, n)
    in_specs = [pl.BlockSpec((tm, k), lambda j, i: (i, 0)),
                pl.BlockSpec((None, k, tn), lambda j, i: (layer, 0, j))]
    args = [a, w]
    if epilogue == "gelu_bias":
        in_specs.append(pl.BlockSpec((1, tn), lambda j, i: (0, j)))
        args.append(extra.reshape(1, n))
    elif epilogue == "residual":
        in_specs.append(pl.BlockSpec((tm, tn), lambda j, i: (i, j)))
        args.append(extra)
    elif epilogue == "headnorm":
        in_specs.append(pl.BlockSpec((1, HEAD_DIM), lambda j, i: (0, 0)))
        args.append(extra.reshape(1, HEAD_DIM))
    return pl.pallas_call(
        functools.partial(_mm_kernel, epilogue=epilogue, scale=scale),
        grid=(n // tn, m // tm),
        in_specs=in_specs,
        out_specs=pl.BlockSpec((tm, tn), lambda j, i: (i, j)),
        out_shape=jax.ShapeDtypeStruct((m, n), out_dtype),
        scratch_shapes=[pltpu.VMEM((k, tn), BF16)],
        name="mm_" + epilogue,
        compiler_params=_params(("parallel", "arbitrary")),
    )(*args)


def _gmlp_gate_kernel(u_ref, v_ref, lng_ref, lnb_ref, ws_ref, bs_ref, o_ref, *, n_chunks, groups):
    v = v_ref[...].astype(F32)
    mu = jnp.mean(v, axis=-1, keepdims=True)
    vc = v - mu
    var = jnp.mean(vc * vc, axis=-1, keepdims=True)
    vn = (vc * lax.rsqrt(var + LN_EPS) * lng_ref[...] + lnb_ref[...]).astype(BF16)
    gw = v.shape[1] // groups
    row = lax.broadcasted_iota(jnp.int32, (CHUNK, CHUNK), 0)
    col = lax.broadcasted_iota(jnp.int32, (CHUNK, CHUNK), 1)
    causal = row >= col
    for g in range(groups):
        w = jnp.where(causal, ws_ref[g], 0.0).astype(BF16)
        b = bs_ref[g]
        for c in range(n_chunks):
            rows = slice(c * CHUNK, (c + 1) * CHUNK)
            cols = slice(g * gw, (g + 1) * gw)
            s = jnp.dot(w, vn[rows, cols], preferred_element_type=F32) + b
            o_ref[rows, cols] = (u_ref[rows, cols].astype(F32) * s).astype(o_ref.dtype)


def _gmlp_gate(z, ln_g, ln_b, w_s, b_s):
    t, d2 = z.shape
    dg = d2 // 2
    groups = w_s.shape[0]
    tm = min(GATE_TM, t)
    return pl.pallas_call(
        functools.partial(_gmlp_gate_kernel, n_chunks=tm // CHUNK, groups=groups),
        grid=(t // tm,),
        in_specs=[pl.BlockSpec((tm, dg), lambda i: (i, 0)),
                  pl.BlockSpec((tm, dg), lambda i: (i, 1)),
                  pl.BlockSpec((1, dg), lambda i: (0, 0)),
                  pl.BlockSpec((1, dg), lambda i: (0, 0)),
                  pl.BlockSpec((groups, CHUNK, CHUNK), lambda i: (0, 0, 0)),
                  pl.BlockSpec((groups, CHUNK, 1), lambda i: (0, 0, 0))],
        out_specs=pl.BlockSpec((tm, dg), lambda i: (i, 0)),
        out_shape=jax.ShapeDtypeStruct((t, dg), BF16),
        name="gmlp_gate",
        compiler_params=_params(("parallel",)),
    )(z, z, ln_g.reshape(1, dg), ln_b.reshape(1, dg), w_s, b_s.reshape(groups, CHUNK, 1))


def _t5_causal_bucket(rel):
    n = jnp.maximum(rel, 0)
    max_exact = N_BUCKETS // 2
    nf = jnp.maximum(n, 1).astype(F32)
    large = max_exact + (jnp.log(nf / max_exact) / math.log(MAX_DISTANCE / max_exact)
                         * (N_BUCKETS - max_exact)).astype(jnp.int32)
    large = jnp.minimum(large, N_BUCKETS - 1)
    return jnp.where(n < max_exact, n, large)


def _attn_kernel(lam_ref, qt_ref, k_ref, vt_ref, bias_ref, subln_ref, o_ref, acc_ref, m_ref, l_ref,
                 *, tq, lambda_init):
    qi = pl.program_id(2)
    m_ref[...] = jnp.full(m_ref.shape, NEG_INF, F32)
    l_ref[...] = jnp.zeros(l_ref.shape, F32)
    acc_ref[...] = jnp.zeros(acc_ref.shape, F32)
    key = lax.broadcasted_iota(jnp.int32, (tq, tq), 0)
    qry = lax.broadcasted_iota(jnp.int32, (tq, tq), 1)
    causal = qry >= key

    def step(j, bias, masked):
        ks = pl.multiple_of(j * tq, tq)
        vt = vt_ref[0, 0, j]
        for c in range(2):
            k = k_ref[0, pl.ds(ks, tq), c * HEAD_DIM:(c + 1) * HEAD_DIM]
            qt = qt_ref[0, c * HEAD_DIM:(c + 1) * HEAD_DIM, :]
            s = jnp.dot(k, qt, preferred_element_type=F32)
            if bias is not None:
                s = s + bias
            if masked:
                s = jnp.where(causal, s, NEG_INF)
            m_prev = m_ref[c]
            m_new = jnp.maximum(m_prev, jnp.max(s, axis=0, keepdims=True))
            alpha = jnp.exp(m_prev - m_new)
            p = jnp.exp(s - m_new)
            l_ref[c] = alpha * l_ref[c] + jnp.sum(p, axis=0, keepdims=True)
            acc_ref[c] = alpha * acc_ref[c] + jnp.dot(vt, p.astype(BF16), preferred_element_type=F32)
            m_ref[c] = m_new

    def far_body(j, carry):
        step(j, None, False)
        return carry

    lax.fori_loop(0, jnp.maximum(qi - 1, 0), far_body, 0)

    @pl.when(qi >= 1)
    def _():
        step(qi - 1, bias_ref[0, 1], False)

    step(qi, bias_ref[0, 0], True)

    lam_v = lam_ref[...]
    lam = (jnp.exp(jnp.sum(lam_v[0:1] * lam_v[1:2], axis=-1, keepdims=True))
           - jnp.exp(jnp.sum(lam_v[2:3] * lam_v[3:4], axis=-1, keepdims=True)) + lambda_init)
    ot = acc_ref[0] / l_ref[0] - lam * (acc_ref[1] / l_ref[1])
    ot = ot * lax.rsqrt(jnp.mean(ot * ot, axis=0, keepdims=True) + EPS)
    ot = ot * subln_ref[...] * (1.0 - lambda_init)
    o_ref[...] = ot.T.astype(o_ref.dtype)


def _toeplitz_bias_tiles(rel_bias, tq):
    n_heads = rel_bias.shape[1]
    table = (rel_bias - rel_bias[N_BUCKETS - 1:N_BUCKETS, :]).astype(F32)
    length = 3 * tq - 1
    rel = jnp.arange(length, dtype=jnp.int32) - (tq - 1)
    onehot = (_t5_causal_bucket(rel)[:, None] == jnp.arange(N_BUCKETS, dtype=jnp.int32)[None, :]).astype(F32)
    u = jnp.dot(onehot, table, precision=lax.Precision.HIGHEST).T
    rep = jnp.tile(u, (1, tq + 1))[:, :tq * (length + 1)].reshape(n_heads, tq, length + 1)
    hankel = rep[:, :, :2 * tq]
    flipped = hankel[:, ::-1, :]
    return jnp.stack([flipped[:, :, :tq], flipped[:, :, tq:]], axis=1)


def _diff_attention(q, k, v, rel_bias, lam_vecs, subln, *, batch, seq, lambda_init):
    t, d = q.shape
    hd2 = 2 * HEAD_DIM
    n_heads = d // hd2
    tq = min(ATTN_TQ, seq)
    nq = seq // tq
    assert tq >= CHUNK and seq % tq == 0
    far = np.arange(tq + 1, max(seq, tq + 2), dtype=np.int64)
    far_bucket = N_BUCKETS // 2 + (np.log(far.astype(np.float32) / (N_BUCKETS // 2))
                                   / math.log(MAX_DISTANCE / (N_BUCKETS // 2))
                                   * (N_BUCKETS - N_BUCKETS // 2)).astype(np.int64)
    assert np.all(far_bucket >= N_BUCKETS - 1)
    bias = _toeplitz_bias_tiles(rel_bias, tq)
    qt = jnp.transpose(q.reshape(batch, seq, d), (0, 2, 1))
    k3 = k.reshape(batch, seq, d)
    vt = jnp.transpose(v.reshape(batch, nq, tq, n_heads, hd2), (0, 3, 1, 4, 2))
    return pl.pallas_call(
        functools.partial(_attn_kernel, tq=tq, lambda_init=lambda_init),
        grid=(batch, n_heads, nq),
        in_specs=[pl.BlockSpec((4, HEAD_DIM), lambda b, h, i: (0, 0)),
                  pl.BlockSpec((1, hd2, tq), lambda b, h, i: (b, h, i)),
                  pl.BlockSpec((1, seq, hd2), lambda b, h, i: (b, 0, h)),
                  pl.BlockSpec((1, 1, nq, hd2, tq), lambda b, h, i: (b, h, 0, 0, 0)),
                  pl.BlockSpec((1, 2, tq, tq), lambda b, h, i: (h, 0, 0, 0)),
                  pl.BlockSpec((hd2, 1), lambda b, h, i: (0, 0))],
        out_specs=pl.BlockSpec((tq, hd2), lambda b, h, i: (b * nq + i, h)),
        out_shape=jax.ShapeDtypeStruct((t, d), BF16),
        scratch_shapes=[pltpu.VMEM((2, hd2, tq), F32),
                        pltpu.VMEM((2, 1, tq), F32),
                        pltpu.VMEM((2, 1, tq), F32)],
        name="diff_attn",
        compiler_params=_params(("parallel", "parallel", "arbitrary")),
    )(lam_vecs, qt, k3, vt, bias, subln.reshape(hd2, 1))


def _router_kernel(x_ref, g_ref, wr_ref, o_ref, cnt_ref, *, n_groups, n_experts):
    @pl.when(pl.program_id(0) == 0)
    def _():
        cnt_ref[...] = jnp.zeros(cnt_ref.shape, F32)

    x = x_ref[...]
    hn = x * lax.rsqrt(jnp.mean(x * x, axis=-1, keepdims=True) + EPS) * g_ref[...]
    logits = jnp.dot(hn, wr_ref[...], preferred_element_type=F32, precision=lax.Precision.HIGHEST)
    lane = lax.broadcasted_iota(jnp.int32, logits.shape, 1).astype(F32)
    no_lane = float(LANES)
    is_g = lane < n_groups
    gl = jnp.where(is_g, logits, -jnp.inf)
    gmax = jnp.max(gl, axis=-1, keepdims=True)
    gidx = jnp.min(jnp.where(gl == gmax, lane, no_lane), axis=-1, keepdims=True)
    gsum = jnp.sum(jnp.where(is_g, jnp.exp(gl - gmax), 0.0), axis=-1, keepdims=True)
    g_gate = 1.0 / gsum
    lo = n_groups + gidx * n_experts
    in_grp = (lane >= lo) & (lane < lo + n_experts)
    el = jnp.where(in_grp, logits, -jnp.inf)
    t1 = jnp.max(el, axis=-1, keepdims=True)
    i1 = jnp.min(jnp.where(el == t1, lane, no_lane), axis=-1, keepdims=True)
    el2 = jnp.where(lane == i1, -jnp.inf, el)
    t2 = jnp.max(el2, axis=-1, keepdims=True)
    i2 = jnp.min(jnp.where(el2 == t2, lane, no_lane), axis=-1, keepdims=True)
    e = jnp.exp(t2 - t1)
    w1 = g_gate / (1.0 + e)
    w2 = g_gate * e / (1.0 + e)
    e1 = i1 - n_groups
    e2 = i2 - n_groups
    tm = logits.shape[0]
    oh1 = lane == e1
    oh2 = lane == e2
    strict_lower = (lax.broadcasted_iota(jnp.int32, (tm, tm), 0)
                    > lax.broadcasted_iota(jnp.int32, (tm, tm), 1)).astype(BF16)
    pre1 = jnp.dot(strict_lower, oh1.astype(BF16), preferred_element_type=F32)
    pre2 = jnp.dot(strict_lower, oh2.astype(BF16), preferred_element_type=F32)
    tot1 = jnp.sum(oh1.astype(F32), axis=0, keepdims=True)
    tot2 = jnp.sum(oh2.astype(F32), axis=0, keepdims=True)
    base = cnt_ref[...]
    rank1 = jnp.sum(jnp.where(oh1, base + pre1, 0.0), axis=-1, keepdims=True)
    rank2 = jnp.sum(jnp.where(oh2, base + tot1 + pre2, 0.0), axis=-1, keepdims=True)
    cnt_ref[...] = base + tot1 + tot2
    out = jnp.zeros(logits.shape, F32)
    for k, val in enumerate((e1, e2, w1, w2, rank1, rank2)):
        out = jnp.where(lane == k, val, out)
    o_ref[...] = out


def _router(h, gain, w_group, w_router):
    t, d = h.shape
    n_groups, _, n_experts = w_router.shape
    tm = min(ROUTER_TM, t)
    assert n_groups + n_groups * n_experts <= LANES
    wr = jnp.concatenate([w_group, jnp.transpose(w_router, (1, 0, 2)).reshape(d, n_groups * n_experts)], axis=1)
    wr = jnp.pad(wr, ((0, 0), (0, LANES - wr.shape[1])))
    return pl.pallas_call(
        functools.partial(_router_kernel, n_groups=n_groups, n_experts=n_experts),
        grid=(t // tm,),
        in_specs=[pl.BlockSpec((tm, d), lambda i: (i, 0)),
                  pl.BlockSpec((1, d), lambda i: (0, 0)),
                  pl.BlockSpec((d, LANES), lambda i: (0, 0))],
        out_specs=[pl.BlockSpec((tm, LANES), lambda i: (i, 0)),
                   pl.BlockSpec((1, LANES), lambda i: (0, 0))],
        out_shape=[jax.ShapeDtypeStruct((t, LANES), F32),
                   jax.ShapeDtypeStruct((1, LANES), F32)],
        name="moe_router",
        compiler_params=_params(("arbitrary",)),
    )(h, gain.reshape(1, d), wr)


def _row_copy(src_hbm, idx, buf, slot, r, sem):
    return pltpu.make_async_copy(src_hbm.at[pl.ds(idx, 1), :], buf.at[slot, pl.ds(r, 1), :], sem.at[slot])


def _gather_start(src_hbm, idx_ref, base, buf, slot, sem, n_rows):
    def body(r, c):
        _row_copy(src_hbm, idx_ref[base + r], buf, slot, r, sem).start()
        return c
    lax.fori_loop(0, n_rows, body, 0, unroll=8)


def _gather_wait(src_hbm, buf, slot, sem, n_rows):
    def body(r, c):
        _row_copy(src_hbm, 0, buf, slot, r, sem).wait()
        return c
    lax.fori_loop(0, n_rows, body, 0, unroll=8)


def _expert_changed(te_ref, t):
    return (t == 0) | (te_ref[t] != te_ref[jnp.maximum(t - 1, 0)])


def _moe_up_kernel(te_ref, nv_ref, tok_ref, h_hbm, g_ref, w1_ref, w3_ref, o_ref, buf, sem, w1b_ref, w3b_ref,
                   *, tm):
    t = pl.program_id(0)
    n_valid = nv_ref[0]
    slot = t % 2

    @pl.when(_expert_changed(te_ref, t))
    def _():
        w1b_ref[...] = w1_ref[0].astype(BF16)
        w3b_ref[...] = w3_ref[0].astype(BF16)

    @pl.when(t == 0)
    def _():
        _gather_start(h_hbm, tok_ref, 0, buf, 0, sem, tm)

    @pl.when(t + 1 < n_valid)
    def _():
        _gather_start(h_hbm, tok_ref, (t + 1) * tm, buf, 1 - slot, sem, tm)

    @pl.when((t < n_valid) | (t == 0))
    def _():
        _gather_wait(h_hbm, buf, slot, sem, tm)

    @pl.when(t < n_valid)
    def _():
        x = buf[slot]
        hn = (x * lax.rsqrt(jnp.mean(x * x, axis=-1, keepdims=True) + EPS) * g_ref[...]).astype(BF16)
        a = jnp.dot(hn, w1b_ref[...], preferred_element_type=F32)
        b = jnp.dot(hn, w3b_ref[...], preferred_element_type=F32)
        o_ref[...] = (a * jax.nn.sigmoid(a) * b).astype(o_ref.dtype)

    @pl.when(t >= n_valid)
    def _():
        o_ref[...] = jnp.zeros(o_ref.shape, o_ref.dtype)


def _moe_down_kernel(te_ref, nv_ref, h_ref, w2_ref, o_ref, w2b_ref):
    t = pl.program_id(0)

    @pl.when(_expert_changed(te_ref, t))
    def _():
        w2b_ref[...] = w2_ref[0].astype(BF16)

    @pl.when(t < nv_ref[0])
    def _():
        o_ref[...] = jnp.dot(h_ref[...], w2b_ref[...], preferred_element_type=F32)

    @pl.when(t >= nv_ref[0])
    def _():
        o_ref[...] = jnp.zeros(o_ref.shape, o_ref.dtype)


def _combine_kernel(pos_ref, y_hbm, h_ref, route_ref, o_ref, buf0, buf1, sem0, sem1, *, tm, n_tok):
    t = pl.program_id(0)
    n = pl.num_programs(0)
    slot = t % 2

    def start(step, s):
        _gather_start(y_hbm, pos_ref, step * tm, buf0, s, sem0, tm)
        _gather_start(y_hbm, pos_ref, n_tok + step * tm, buf1, s, sem1, tm)

    @pl.when(t == 0)
    def _():
        start(0, 0)

    @pl.when(t + 1 < n)
    def _():
        start(t + 1, 1 - slot)

    _gather_wait(y_hbm, buf0, slot, sem0, tm)
    _gather_wait(y_hbm, buf1, slot, sem1, tm)
    w = route_ref[...]
    o_ref[...] = h_ref[...] + w[:, 2:3] * buf0[slot] + w[:, 3:4] * buf1[slot]


def _hier_moe(h, gain, w_group, w_router, w1, w3, w2, layer):
    t, d = h.shape
    f = w1.shape[2]
    ne = w_router.shape[0] * w_router.shape[2]
    e0 = layer * ne
    tm, tc = MOE_TM, COMBINE_TM
    route, counts = _router(h, gain, w_group, w_router)
    e_pair = jnp.concatenate([route[:, 0], route[:, 1]]).astype(jnp.int32)
    rank = jnp.concatenate([route[:, 4], route[:, 5]]).astype(jnp.int32)
    tok_pair = jnp.tile(jnp.arange(t, dtype=jnp.int32), TOP_K)

    counts = counts[0, :ne].astype(jnp.int32)
    tiles_e = (counts + tm - 1) // tm
    tile_end = jnp.cumsum(tiles_e)
    row_start = (tile_end - tiles_e) * tm
    is_e = e_pair[:, None] == jnp.arange(ne, dtype=jnp.int32)[None, :]
    dest = jnp.sum(jnp.where(is_e, row_start[None, :], 0), axis=1) + rank
    n_tiles = (TOP_K * t) // tm + ne
    n_rows = n_tiles * tm
    tok_sorted = jnp.zeros((n_rows,), jnp.int32).at[dest].set(tok_pair)
    n_valid = tile_end[-1]
    tile_ids = jnp.minimum(jnp.arange(n_tiles, dtype=jnp.int32), n_valid - 1)
    tile_expert = jnp.searchsorted(tile_end, tile_ids, side="right").astype(jnp.int32)
    n_valid = n_valid.reshape(1).astype(jnp.int32)

    hid = pl.pallas_call(
        functools.partial(_moe_up_kernel, tm=tm),
        grid_spec=pltpu.PrefetchScalarGridSpec(
            num_scalar_prefetch=3,
            grid=(n_tiles,),
            in_specs=[pl.BlockSpec(memory_space=pl.ANY),
                      pl.BlockSpec((1, d), lambda i, te, nv, tok: (0, 0)),
                      pl.BlockSpec((1, d, f), lambda i, te, nv, tok: (e0 + te[i], 0, 0)),
                      pl.BlockSpec((1, d, f), lambda i, te, nv, tok: (e0 + te[i], 0, 0))],
            out_specs=pl.BlockSpec((tm, f), lambda i, te, nv, tok: (i, 0)),
            scratch_shapes=[pltpu.VMEM((2, tm, d), F32), pltpu.SemaphoreType.DMA((2,)),
                            pltpu.VMEM((d, f), BF16), pltpu.VMEM((d, f), BF16)]),
        out_shape=jax.ShapeDtypeStruct((n_rows, f), BF16),
        name="moe_up",
        compiler_params=_params(("arbitrary",)),
    )(tile_expert, n_valid, tok_sorted, h, gain.reshape(1, d), w1, w3)

    ys = pl.pallas_call(
        _moe_down_kernel,
        grid_spec=pltpu.PrefetchScalarGridSpec(
            num_scalar_prefetch=2,
            grid=(n_tiles,),
            in_specs=[pl.BlockSpec((tm, f), lambda i, te, nv: (i, 0)),
                      pl.BlockSpec((1, f, d), lambda i, te, nv: (e0 + te[i], 0, 0))],
            out_specs=pl.BlockSpec((tm, d), lambda i, te, nv: (i, 0)),
            scratch_shapes=[pltpu.VMEM((f, d), BF16)]),
        out_shape=jax.ShapeDtypeStruct((n_rows, d), F32),
        name="moe_down",
        compiler_params=_params(("arbitrary",)),
    )(tile_expert, n_valid, hid, w2)

    return pl.pallas_call(
        functools.partial(_combine_kernel, tm=tc, n_tok=t),
        grid_spec=pltpu.PrefetchScalarGridSpec(
            num_scalar_prefetch=1,
            grid=(t // tc,),
            in_specs=[pl.BlockSpec(memory_space=pl.ANY),
                      pl.BlockSpec((tc, d), lambda i, pos: (i, 0)),
                      pl.BlockSpec((tc, LANES), lambda i, pos: (i, 0))],
            out_specs=pl.BlockSpec((tc, d), lambda i, pos: (i, 0)),
            scratch_shapes=[pltpu.VMEM((2, tc, d), F32), pltpu.VMEM((2, tc, d), F32),
                            pltpu.SemaphoreType.DMA((2,)), pltpu.SemaphoreType.DMA((2,))]),
        out_shape=jax.ShapeDtypeStruct((t, d), F32),
        name="moe_combine",
        compiler_params=_params(("arbitrary",)),
    )(dest.astype(jnp.int32), ys, h, route)


def kernel(x, a_norm, a_w_in, a_b_in, a_ln_g, a_ln_b, a_w_s, a_b_s, a_w_out, kv_norm, w_k, w_v, k_norm, b_norm, w_q, q_norm, lam_q1, lam_k1, lam_q2, lam_k2, subln, w_o, rel_bias, m_norm, m_w_group, m_w_router, m_w1, m_w3, m_w2):
    batch, seq, d = x.shape
    depth = m_norm.shape[0]
    n_a = a_norm.shape[0]
    h = x.reshape(batch * seq, d)
    f = m_w1.shape[-1]
    w1 = m_w1.reshape(-1, d, f)
    w3 = m_w3.reshape(-1, d, f)
    w2 = m_w2.reshape(-1, f, d)
    k = v = None
    for layer in range(depth):
        if layer < n_a:
            i = layer
            (hn,) = _rmsnorm(h, a_norm[i:i + 1])
            z = _matmul(hn, a_w_in, i, epilogue="gelu_bias", extra=a_b_in[i])
            gz = _gmlp_gate(z, a_ln_g[i], a_ln_b[i], a_w_s[i], a_b_s[i])
            h = _matmul(gz, a_w_out, i, epilogue="residual", extra=h, out_dtype=F32)
        else:
            j = layer - n_a
            if layer == n_a:
                hkv, hq = _rmsnorm(h, jnp.stack([kv_norm, b_norm[j]]))
                k = _matmul(hkv, w_k, epilogue="headnorm", extra=k_norm)
                v = _matmul(hkv, w_v)
            else:
                (hq,) = _rmsnorm(h, b_norm[j:j + 1])
            q = _matmul(hq, w_q, j, epilogue="headnorm", extra=q_norm[j], scale=HEAD_DIM ** -0.5)
            lam_vecs = jnp.stack([lam_q1[j], lam_k1[j], lam_q2[j], lam_k2[j]]).astype(F32)
            o = _diff_attention(q, k, v, rel_bias, lam_vecs, subln[j], batch=batch, seq=seq,
                                lambda_init=_lambda_init(layer))
            h = _matmul(o, w_o, j, epilogue="residual", extra=h, out_dtype=F32)
        h = _hier_moe(h, m_norm[layer], m_w_group[layer], m_w_router[layer], w1, w3, w2, layer)
    return h.reshape(batch, seq, d)
```

```python
import functools
import math

import jax
import jax.numpy as jnp
import numpy as np
from jax import lax
from jax.experimental import pallas as pl
from jax.experimental.pallas import tpu as pltpu

F32 = jnp.float32
BF16 = jnp.bfloat16

CHUNK = 128
HEAD_DIM = 128
N_BUCKETS = 32
MAX_DISTANCE = 128
TOP_K = 2
EPS = 1e-6
LN_EPS = 1e-5
NEG_INF = -1e30
FINITE_MAX = 3.0e38

LANES = 128
VMEM_LIMIT_BYTES = 56 * 1024 * 1024

NORM_TM = 512
MM_TM, MM_TN = 512, 1024
GATE_TM = 512
ATTN_TQ, ATTN_TK = 1024, 512
ROUTER_TM = 512
MOE_TM = 256
COMBINE_TM = 128


def _params(semantics):
    return pltpu.CompilerParams(dimension_semantics=semantics, vmem_limit_bytes=VMEM_LIMIT_BYTES)


def _lambda_init(layer):
    return 0.8 - 0.6 * math.exp(-0.3 * layer)


def _rmsnorm_kernel(x_ref, g_ref, *o_refs):
    x = x_ref[...]
    y = x * lax.rsqrt(jnp.mean(x * x, axis=-1, keepdims=True) + EPS)
    for k, o_ref in enumerate(o_refs):
        o_ref[...] = (y * g_ref[k:k + 1, :]).astype(o_ref.dtype)


def _rmsnorm(x, gains):
    t, d = x.shape
    n = gains.shape[0]
    tm = min(NORM_TM, t)
    return pl.pallas_call(
        _rmsnorm_kernel,
        grid=(t // tm,),
        in_specs=[pl.BlockSpec((tm, d), lambda i: (i, 0)),
                  pl.BlockSpec((n, d), lambda i: (0, 0))],
        out_specs=[pl.BlockSpec((tm, d), lambda i: (i, 0))] * n,
        out_shape=[jax.ShapeDtypeStruct((t, d), BF16)] * n,
        name="rmsnorm",
        compiler_params=_params(("parallel",)),
    )(x, gains)


def _gelu_tanh(x):
    return 0.5 * x * (1.0 + jnp.tanh(math.sqrt(2.0 / math.pi) * (x + 0.044715 * (x * x * x))))


def _mm_kernel(a_ref, w_ref, *rest, epilogue, scale):
    o_ref, wb_ref = rest[-2], rest[-1]

    @pl.when(pl.program_id(1) == 0)
    def _():
        wb_ref[...] = w_ref[...].astype(BF16)

    acc = jnp.dot(a_ref[...], wb_ref[...], preferred_element_type=F32)
    if epilogue == "gelu_bias":
        o_ref[...] = _gelu_tanh(acc + rest[0][...]).astype(o_ref.dtype)
    elif epilogue == "residual":
        o_ref[...] = rest[0][...] + acc
    elif epilogue == "headnorm":
        gain = rest[0][...] * scale
        for g in range(acc.shape[1] // HEAD_DIM):
            y = acc[:, g * HEAD_DIM:(g + 1) * HEAD_DIM]
            y = y * lax.rsqrt(jnp.mean(y * y, axis=-1, keepdims=True) + EPS)
            o_ref[:, g * HEAD_DIM:(g + 1) * HEAD_DIM] = (y * gain).astype(o_ref.dtype)
    else:
        o_ref[...] = acc.astype(o_ref.dtype)


def _matmul(a, w, layer=0, *, epilogue="none", extra=None, scale=1.0, out_dtype=BF16):
    m, k = a.shape
    if w.ndim == 2:
        w = w.reshape(1, *w.shape)
    n = w.shape[2]
    tm, tn = min(MM_TM, m), min(MM_TN, n)
    in_specs = [pl.BlockSpec((tm, k), lambda j, i: (i, 0)),
                pl.BlockSpec((None, k, tn), lambda j, i: (layer, 0, j), pipeline_mode=pl.Buffered(1))]
    args = [a, w]
    if epilogue == "gelu_bias":
        in_specs.append(pl.BlockSpec((1, tn), lambda j, i: (0, j)))
        args.append(extra.reshape(1, n))
    elif epilogue == "residual":
        in_specs.append(pl.BlockSpec((tm, tn), lambda j, i: (i, j)))
        args.append(extra)
    elif epilogue == "headnorm":
        in_specs.append(pl.BlockSpec((1, HEAD_DIM), lambda j, i: (0, 0)))
        args.append(extra.reshape(1, HEAD_DIM))
    return pl.pallas_call(
        functools.partial(_mm_kernel, epilogue=epilogue, scale=scale),
        grid=(n // tn, m // tm),
        in_specs=in_specs,
        out_specs=pl.BlockSpec((tm, tn), lambda j, i: (i, j)),
        out_shape=jax.ShapeDtypeStruct((m, n), out_dtype),
        scratch_shapes=[pltpu.VMEM((k, tn), BF16)],
        name="mm_" + epilogue,
        compiler_params=_params(("parallel", "arbitrary")),
    )(*args)


def _gmlp_gate_kernel(u_ref, v_ref, lng_ref, lnb_ref, ws_ref, bs_ref, o_ref, *, n_chunks, groups):
    v = v_ref[...].astype(F32)
    mu = jnp.mean(v, axis=-1, keepdims=True)
    vc = v - mu
    var = jnp.mean(vc * vc, axis=-1, keepdims=True)
    vn = (vc * lax.rsqrt(var + LN_EPS) * lng_ref[...] + lnb_ref[...]).astype(BF16)
    gw = v.shape[1] // groups
    row = lax.broadcasted_iota(jnp.int32, (CHUNK, CHUNK), 0)
    col = lax.broadcasted_iota(jnp.int32, (CHUNK, CHUNK), 1)
    causal = row >= col
    for g in range(groups):
        w = jnp.where(causal, ws_ref[g], 0.0).astype(BF16)
        b = bs_ref[g]
        for c in range(n_chunks):
            rows = slice(c * CHUNK, (c + 1) * CHUNK)
            cols = slice(g * gw, (g + 1) * gw)
            s = jnp.dot(w, vn[rows, cols], preferred_element_type=F32) + b
            o_ref[rows, cols] = (u_ref[rows, cols].astype(F32) * s).astype(o_ref.dtype)


def _gmlp_gate(z, ln_g, ln_b, w_s, b_s):
    t, d2 = z.shape
    dg = d2 // 2
    groups = w_s.shape[0]
    tm = min(GATE_TM, t)
    return pl.pallas_call(
        functools.partial(_gmlp_gate_kernel, n_chunks=tm // CHUNK, groups=groups),
        grid=(t // tm,),
        in_specs=[pl.BlockSpec((tm, dg), lambda i: (i, 0)),
                  pl.BlockSpec((tm, dg), lambda i: (i, 1)),
                  pl.BlockSpec((1, dg), lambda i: (0, 0)),
                  pl.BlockSpec((1, dg), lambda i: (0, 0)),
                  pl.BlockSpec((groups, CHUNK, CHUNK), lambda i: (0, 0, 0)),
                  pl.BlockSpec((groups, CHUNK, 1), lambda i: (0, 0, 0))],
        out_specs=pl.BlockSpec((tm, dg), lambda i: (i, 0)),
        out_shape=jax.ShapeDtypeStruct((t, dg), BF16),
        name="gmlp_gate",
        compiler_params=_params(("parallel",)),
    )(z, z, ln_g.reshape(1, dg), ln_b.reshape(1, dg), w_s, b_s.reshape(groups, CHUNK, 1))


def _t5_causal_bucket(rel):
    n = jnp.maximum(rel, 0)
    max_exact = N_BUCKETS // 2
    nf = jnp.maximum(n, 1).astype(F32)
    large = max_exact + (jnp.log(nf / max_exact) / math.log(MAX_DISTANCE / max_exact)
                         * (N_BUCKETS - max_exact)).astype(jnp.int32)
    large = jnp.minimum(large, N_BUCKETS - 1)
    return jnp.where(n < max_exact, n, large)


def _attn_kernel(lam_ref, qt_ref, k_ref, vt_ref, bias_ref, subln_ref, o_ref, acc_ref, m_ref, l_ref,
                 *, tq, tk, lambda_init):
    qi = pl.program_id(2)
    ratio = tq // tk
    first_near = ratio * qi - 1
    key = lax.broadcasted_iota(jnp.int32, (tk, tq), 0)
    qry = lax.broadcasted_iota(jnp.int32, (tk, tq), 1)

    def scores(j, c, near):
        ks = pl.multiple_of(j * tk, tk)
        k = k_ref[0, pl.ds(ks, tk), c * HEAD_DIM:(c + 1) * HEAD_DIM]
        qt = qt_ref[0, c * HEAD_DIM:(c + 1) * HEAD_DIM, :]
        s = jnp.dot(k, qt, preferred_element_type=F32)
        if near is not None:
            d = near * tk
            lo = d + tq - tk
            s = s + bias_ref[0, :, lo:lo + tq]
            if d <= 0:
                s = jnp.where(qry + d >= key, s, NEG_INF)
        return s

    def fixed_step(j, near):
        vt = vt_ref[0, 0, j]
        for c in range(2):
            p = jnp.exp(scores(j, c, near) - m_ref[c])
            l_ref[c] += jnp.sum(p, axis=0, keepdims=True)
            acc_ref[c] += jnp.dot(vt, p.astype(BF16), preferred_element_type=F32)

    def online_step(j, near):
        vt = vt_ref[0, 0, j]
        for c in range(2):
            s = scores(j, c, near)
            m_prev = m_ref[c]
            m_new = jnp.maximum(m_prev, jnp.max(s, axis=0, keepdims=True))
            alpha = jnp.exp(m_prev - m_new)
            p = jnp.exp(s - m_new)
            l_ref[c] = alpha * l_ref[c] + jnp.sum(p, axis=0, keepdims=True)
            acc_ref[c] = alpha * acc_ref[c] + jnp.dot(vt, p.astype(BF16), preferred_element_type=F32)
            m_ref[c] = m_new

    def all_blocks(step):
        def far_body(j, carry):
            step(j, None)
            return carry
        lax.fori_loop(0, jnp.maximum(first_near, 0), far_body, 0)

        @pl.when(qi >= 1)
        def _():
            step(first_near, 1)

        for r in range(ratio):
            step(first_near + 1 + r, -r)

    l_ref[...] = jnp.zeros(l_ref.shape, F32)
    acc_ref[...] = jnp.zeros(acc_ref.shape, F32)
    self_bias = bias_ref[0, 0:1, tq - tk:tq - tk + 1]
    q0 = pl.multiple_of(qi * tq, tq)
    for c in range(2):
        k_self = k_ref[0, pl.ds(q0, tq), c * HEAD_DIM:(c + 1) * HEAD_DIM].astype(F32)
        qt = qt_ref[0, c * HEAD_DIM:(c + 1) * HEAD_DIM, :].astype(F32)
        m_ref[c] = jnp.sum(qt * k_self.T, axis=0, keepdims=True) + self_bias
    all_blocks(fixed_step)

    n_bad = (jnp.sum(jnp.where(jnp.abs(acc_ref[...]) < FINITE_MAX, 0.0, 1.0))
             + jnp.sum(jnp.where(jnp.abs(l_ref[...]) < FINITE_MAX, 0.0, 1.0)))

    @pl.when(n_bad > 0.0)
    def _():
        m_ref[...] = jnp.full(m_ref.shape, NEG_INF, F32)
        l_ref[...] = jnp.zeros(l_ref.shape, F32)
        acc_ref[...] = jnp.zeros(acc_ref.shape, F32)
        all_blocks(online_step)

    lam_v = lam_ref[...]
    lam = (jnp.exp(jnp.sum(lam_v[0:1] * lam_v[1:2], axis=-1, keepdims=True))
           - jnp.exp(jnp.sum(lam_v[2:3] * lam_v[3:4], axis=-1, keepdims=True)) + lambda_init)
    ot = acc_ref[0] / l_ref[0] - lam * (acc_ref[1] / l_ref[1])
    ot = ot * lax.rsqrt(jnp.mean(ot * ot, axis=0, keepdims=True) + EPS)
    ot = ot * subln_ref[...] * (1.0 - lambda_init)
    o_ref[...] = ot.T.astype(o_ref.dtype)


def _toeplitz_bias(rel_bias, tq, tk):
    n_heads = rel_bias.shape[1]
    table = (rel_bias - rel_bias[N_BUCKETS - 1:N_BUCKETS, :]).astype(F32)
    length = 2 * tq + tk - 1
    rel = jnp.arange(length, dtype=jnp.int32) - (tq - 1)
    onehot = (_t5_causal_bucket(rel)[:, None] == jnp.arange(N_BUCKETS, dtype=jnp.int32)[None, :]).astype(F32)
    u = jnp.dot(onehot, table, precision=lax.Precision.HIGHEST).T
    u2 = jnp.roll(u, -(tk - 1), axis=1)
    rep = jnp.tile(u2, (1, tk))[:, :tk * (length - 1)].reshape(n_heads, tk, length - 1)
    return rep[:, :, :2 * tq]


def _diff_attention(q, k, v, rel_bias, lam_vecs, subln, *, batch, seq, lambda_init):
    t, d = q.shape
    hd2 = 2 * HEAD_DIM
    n_heads = d // hd2
    tq = min(ATTN_TQ, seq)
    tk = min(ATTN_TK, tq)
    nq, nk = seq // tq, seq // tk
    assert tk >= CHUNK and tq % tk == 0 and seq % tq == 0
    far = np.arange(tk + 1, max(seq, tk + 2), dtype=np.int64)
    far_bucket = N_BUCKETS // 2 + (np.log(far.astype(np.float32) / (N_BUCKETS // 2))
                                   / math.log(MAX_DISTANCE / (N_BUCKETS // 2))
                                   * (N_BUCKETS - N_BUCKETS // 2)).astype(np.int64)
    assert np.all(far_bucket >= N_BUCKETS - 1)
    bias = _toeplitz_bias(rel_bias, tq, tk)
    qt = jnp.transpose(q.reshape(batch, seq, d), (0, 2, 1))
    k3 = k.reshape(batch, seq, d)
    vt = jnp.transpose(v.reshape(batch, nk, tk, n_heads, hd2), (0, 3, 1, 4, 2))
    return pl.pallas_call(
        functools.partial(_attn_kernel, tq=tq, tk=tk, lambda_init=lambda_init),
        grid=(batch, n_heads, nq),
        in_specs=[pl.BlockSpec((4, HEAD_DIM), lambda b, h, i: (0, 0)),
                  pl.BlockSpec((1, hd2, tq), lambda b, h, i: (b, h, i)),
                  pl.BlockSpec((1, seq, hd2), lambda b, h, i: (b, 0, h)),
                  pl.BlockSpec((1, 1, nk, hd2, tk), lambda b, h, i: (b, h, 0, 0, 0)),
                  pl.BlockSpec((1, tk, 2 * tq), lambda b, h, i: (h, 0, 0)),
                  pl.BlockSpec((hd2, 1), lambda b, h, i: (0, 0))],
        out_specs=pl.BlockSpec((tq, hd2), lambda b, h, i: (b * nq + i, h)),
        out_shape=jax.ShapeDtypeStruct((t, d), BF16),
        scratch_shapes=[pltpu.VMEM((2, hd2, tq), F32),
                        pltpu.VMEM((2, 1, tq), F32),
                        pltpu.VMEM((2, 1, tq), F32)],
        name="diff_attn",
        compiler_params=_params(("parallel", "parallel", "arbitrary")),
    )(lam_vecs, qt, k3, vt, bias, subln.reshape(hd2, 1))


def _router_kernel(x_ref, g_ref, wr_ref, o_ref, cnt_ref, *, n_groups, n_experts):
    @pl.when(pl.program_id(0) == 0)
    def _():
        cnt_ref[...] = jnp.zeros(cnt_ref.shape, F32)

    x = x_ref[...]
    hn = x * lax.rsqrt(jnp.mean(x * x, axis=-1, keepdims=True) + EPS) * g_ref[...]
    logits = jnp.dot(hn, wr_ref[...], preferred_element_type=F32, precision=lax.Precision.HIGHEST)
    lane = lax.broadcasted_iota(jnp.int32, logits.shape, 1).astype(F32)
    no_lane = float(LANES)
    is_g = lane < n_groups
    gl = jnp.where(is_g, logits, -jnp.inf)
    gmax = jnp.max(gl, axis=-1, keepdims=True)
    gidx = jnp.min(jnp.where(gl == gmax, lane, no_lane), axis=-1, keepdims=True)
    gsum = jnp.sum(jnp.where(is_g, jnp.exp(gl - gmax), 0.0), axis=-1, keepdims=True)
    g_gate = 1.0 / gsum
    lo = n_groups + gidx * n_experts
    in_grp = (lane >= lo) & (lane < lo + n_experts)
    el = jnp.where(in_grp, logits, -jnp.inf)
    t1 = jnp.max(el, axis=-1, keepdims=True)
    i1 = jnp.min(jnp.where(el == t1, lane, no_lane), axis=-1, keepdims=True)
    el2 = jnp.where(lane == i1, -jnp.inf, el)
    t2 = jnp.max(el2, axis=-1, keepdims=True)
    i2 = jnp.min(jnp.where(el2 == t2, lane, no_lane), axis=-1, keepdims=True)
    e = jnp.exp(t2 - t1)
    w1 = g_gate / (1.0 + e)
    w2 = g_gate * e / (1.0 + e)
    e1 = i1 - n_groups
    e2 = i2 - n_groups
    tm = logits.shape[0]
    oh1 = lane == e1
    oh2 = lane == e2
    strict_lower = (lax.broadcasted_iota(jnp.int32, (tm, tm), 0)
                    > lax.broadcasted_iota(jnp.int32, (tm, tm), 1)).astype(BF16)
    pre1 = jnp.dot(strict_lower, oh1.astype(BF16), preferred_element_type=F32)
    pre2 = jnp.dot(strict_lower, oh2.astype(BF16), preferred_element_type=F32)
    tot1 = jnp.sum(oh1.astype(F32), axis=0, keepdims=True)
    tot2 = jnp.sum(oh2.astype(F32), axis=0, keepdims=True)
    base = cnt_ref[...]
    rank1 = jnp.sum(jnp.where(oh1, base + pre1, 0.0), axis=-1, keepdims=True)
    rank2 = jnp.sum(jnp.where(oh2, base + tot1 + pre2, 0.0), axis=-1, keepdims=True)
    cnt_ref[...] = base + tot1 + tot2
    out = jnp.zeros(logits.shape, F32)
    for k, val in enumerate((e1, e2, w1, w2, rank1, rank2)):
        out = jnp.where(lane == k, val, out)
    o_ref[...] = out


def _router(h, gain, w_group, w_router):
    t, d = h.shape
    n_groups, _, n_experts = w_router.shape
    tm = min(ROUTER_TM, t)
    assert n_groups + n_groups * n_experts <= LANES
    wr = jnp.concatenate([w_group, jnp.transpose(w_router, (1, 0, 2)).reshape(d, n_groups * n_experts)], axis=1)
    wr = jnp.pad(wr, ((0, 0), (0, LANES - wr.shape[1])))
    return pl.pallas_call(
        functools.partial(_router_kernel, n_groups=n_groups, n_experts=n_experts),
        grid=(t // tm,),
        in_specs=[pl.BlockSpec((tm, d), lambda i: (i, 0)),
                  pl.BlockSpec((1, d), lambda i: (0, 0)),
                  pl.BlockSpec((d, LANES), lambda i: (0, 0))],
        out_specs=[pl.BlockSpec((tm, LANES), lambda i: (i, 0)),
                   pl.BlockSpec((1, LANES), lambda i: (0, 0))],
        out_shape=[jax.ShapeDtypeStruct((t, LANES), F32),
                   jax.ShapeDtypeStruct((1, LANES), F32)],
        name="moe_router",
        compiler_params=_params(("arbitrary",)),
    )(h, gain.reshape(1, d), wr)


def _row_copy(src_hbm, idx, buf, slot, r, sem):
    return pltpu.make_async_copy(src_hbm.at[pl.ds(idx, 1), :], buf.at[slot, pl.ds(r, 1), :], sem.at[slot])


def _gather_start(src_hbm, idx_ref, base, buf, slot, sem, n_rows):
    def body(r, c):
        _row_copy(src_hbm, idx_ref[base + r], buf, slot, r, sem).start()
        return c
    lax.fori_loop(0, n_rows, body, 0, unroll=8)


def _gather_wait(src_hbm, buf, slot, sem, n_rows):
    def body(r, c):
        _row_copy(src_hbm, 0, buf, slot, r, sem).wait()
        return c
    lax.fori_loop(0, n_rows, body, 0, unroll=8)


def _expert_changed(te_ref, t):
    return (t == 0) | (te_ref[t] != te_ref[jnp.maximum(t - 1, 0)])


def _moe_up_kernel(te_ref, nv_ref, tok_ref, h_hbm, g_ref, w1_ref, w3_ref, o_ref, buf, sem, w1b_ref, w3b_ref,
                   *, tm):
    t = pl.program_id(0)
    n_valid = nv_ref[0]
    slot = t % 2

    @pl.when(_expert_changed(te_ref, t))
    def _():
        w1b_ref[...] = w1_ref[0].astype(BF16)
        w3b_ref[...] = w3_ref[0].astype(BF16)

    @pl.when(t == 0)
    def _():
        _gather_start(h_hbm, tok_ref, 0, buf, 0, sem, tm)

    @pl.when(t + 1 < n_valid)
    def _():
        _gather_start(h_hbm, tok_ref, (t + 1) * tm, buf, 1 - slot, sem, tm)

    @pl.when((t < n_valid) | (t == 0))
    def _():
        _gather_wait(h_hbm, buf, slot, sem, tm)

    @pl.when(t < n_valid)
    def _():
        x = buf[slot]
        hn = (x * lax.rsqrt(jnp.mean(x * x, axis=-1, keepdims=True) + EPS) * g_ref[...]).astype(BF16)
        a = jnp.dot(hn, w1b_ref[...], preferred_element_type=F32)
        b = jnp.dot(hn, w3b_ref[...], preferred_element_type=F32)
        o_ref[...] = (a * jax.nn.sigmoid(a) * b).astype(o_ref.dtype)

    @pl.when(t >= n_valid)
    def _():
        o_ref[...] = jnp.zeros(o_ref.shape, o_ref.dtype)


def _moe_down_kernel(te_ref, nv_ref, h_ref, w2_ref, o_ref, w2b_ref):
    t = pl.program_id(0)

    @pl.when(_expert_changed(te_ref, t))
    def _():
        w2b_ref[...] = w2_ref[0].astype(BF16)

    @pl.when(t < nv_ref[0])
    def _():
        o_ref[...] = jnp.dot(h_ref[...], w2b_ref[...], preferred_element_type=F32)

    @pl.when(t >= nv_ref[0])
    def _():
        o_ref[...] = jnp.zeros(o_ref.shape, o_ref.dtype)


def _combine_kernel(pos_ref, y_hbm, h_ref, route_ref, o_ref, buf0, buf1, sem0, sem1, *, tm, n_tok):
    t = pl.program_id(0)
    n = pl.num_programs(0)
    slot = t % 2

    def start(step, s):
        _gather_start(y_hbm, pos_ref, step * tm, buf0, s, sem0, tm)
        _gather_start(y_hbm, pos_ref, n_tok + step * tm, buf1, s, sem1, tm)

    @pl.when(t == 0)
    def _():
        start(0, 0)

    @pl.when(t + 1 < n)
    def _():
        start(t + 1, 1 - slot)

    _gather_wait(y_hbm, buf0, slot, sem0, tm)
    _gather_wait(y_hbm, buf1, slot, sem1, tm)
    w = route_ref[...]
    o_ref[...] = h_ref[...] + w[:, 2:3] * buf0[slot] + w[:, 3:4] * buf1[slot]


def _hier_moe(h, gain, w_group, w_router, w1, w3, w2, layer):
    t, d = h.shape
    f = w1.shape[2]
    ne = w_router.shape[0] * w_router.shape[2]
    e0 = layer * ne
    tm, tc = MOE_TM, COMBINE_TM
    route, counts = _router(h, gain, w_group, w_router)
    e_pair = jnp.concatenate([route[:, 0], route[:, 1]]).astype(jnp.int32)
    rank = jnp.concatenate([route[:, 4], route[:, 5]]).astype(jnp.int32)
    tok_pair = jnp.tile(jnp.arange(t, dtype=jnp.int32), TOP_K)

    counts = counts[0, :ne].astype(jnp.int32)
    tiles_e = (counts + tm - 1) // tm
    tile_end = jnp.cumsum(tiles_e)
    row_start = (tile_end - tiles_e) * tm
    is_e = e_pair[:, None] == jnp.arange(ne, dtype=jnp.int32)[None, :]
    dest = jnp.sum(jnp.where(is_e, row_start[None, :], 0), axis=1) + rank
    n_tiles = (TOP_K * t) // tm + ne
    n_rows = n_tiles * tm
    tok_sorted = jnp.zeros((n_rows,), jnp.int32).at[dest].set(tok_pair)
    n_valid = tile_end[-1]
    tile_ids = jnp.minimum(jnp.arange(n_tiles, dtype=jnp.int32), n_valid - 1)
    tile_expert = jnp.sum((tile_end[None, :] <= tile_ids[:, None]).astype(jnp.int32), axis=1)
    n_valid = n_valid.reshape(1).astype(jnp.int32)

    hid = pl.pallas_call(
        functools.partial(_moe_up_kernel, tm=tm),
        grid_spec=pltpu.PrefetchScalarGridSpec(
            num_scalar_prefetch=3,
            grid=(n_tiles,),
            in_specs=[pl.BlockSpec(memory_space=pl.ANY),
                      pl.BlockSpec((1, d), lambda i, te, nv, tok: (0, 0)),
                      pl.BlockSpec((1, d, f), lambda i, te, nv, tok: (e0 + te[i], 0, 0)),
                      pl.BlockSpec((1, d, f), lambda i, te, nv, tok: (e0 + te[i], 0, 0))],
            out_specs=pl.BlockSpec((tm, f), lambda i, te, nv, tok: (i, 0)),
            scratch_shapes=[pltpu.VMEM((2, tm, d), F32), pltpu.SemaphoreType.DMA((2,)),
                            pltpu.VMEM((d, f), BF16), pltpu.VMEM((d, f), BF16)]),
        out_shape=jax.ShapeDtypeStruct((n_rows, f), BF16),
        name="moe_up",
        compiler_params=_params(("arbitrary",)),
    )(tile_expert, n_valid, tok_sorted, h, gain.reshape(1, d), w1, w3)

    ys = pl.pallas_call(
        _moe_down_kernel,
        grid_spec=pltpu.PrefetchScalarGridSpec(
            num_scalar_prefetch=2,
            grid=(n_tiles,),
            in_specs=[pl.BlockSpec((tm, f), lambda i, te, nv: (i, 0)),
                      pl.BlockSpec((1, f, d), lambda i, te, nv: (e0 + te[i], 0, 0))],
            out_specs=pl.BlockSpec((tm, d), lambda i, te, nv: (i, 0)),
            scratch_shapes=[pltpu.VMEM((f, d), BF16)]),
        out_shape=jax.ShapeDtypeStruct((n_rows, d), F32),
        name="moe_down",
        compiler_params=_params(("arbitrary",)),
    )(tile_expert, n_valid, hid, w2)

    return pl.pallas_call(
        functools.partial(_combine_kernel, tm=tc, n_tok=t),
        grid_spec=pltpu.PrefetchScalarGridSpec(
            num_scalar_prefetch=1,
            grid=(t // tc,),
            in_specs=[pl.BlockSpec(memory_space=pl.ANY),
                      pl.BlockSpec((tc, d), lambda i, pos: (i, 0)),
                      pl.BlockSpec((tc, LANES), lambda i, pos: (i, 0))],
            out_specs=pl.BlockSpec((tc, d), lambda i, pos: (i, 0)),
            scratch_shapes=[pltpu.VMEM((2, tc, d), F32), pltpu.VMEM((2, tc, d), F32),
                            pltpu.SemaphoreType.DMA((2,)), pltpu.SemaphoreType.DMA((2,))]),
        out_shape=jax.ShapeDtypeStruct((t, d), F32),
        name="moe_combine",
        compiler_params=_params(("arbitrary",)),
    )(dest.astype(jnp.int32), ys, h, route)


def kernel(x, a_norm, a_w_in, a_b_in, a_ln_g, a_ln_b, a_w_s, a_b_s, a_w_out, kv_norm, w_k, w_v, k_norm, b_norm, w_q, q_norm, lam_q1, lam_k1, lam_q2, lam_k2, subln, w_o, rel_bias, m_norm, m_w_group, m_w_router, m_w1, m_w3, m_w2):
    batch, seq, d = x.shape
    depth = m_norm.shape[0]
    n_a = a_norm.shape[0]
    h = x.reshape(batch * seq, d)
    f = m_w1.shape[-1]
    w1 = m_w1.reshape(-1, d, f)
    w3 = m_w3.reshape(-1, d, f)
    w2 = m_w2.reshape(-1, f, d)
    k = v = None
    for layer in range(depth):
        if layer < n_a:
            i = layer
            (hn,) = _rmsnorm(h, a_norm[i:i + 1])
            z = _matmul(hn, a_w_in, i, epilogue="gelu_bias", extra=a_b_in[i])
            gz = _gmlp_gate(z, a_ln_g[i], a_ln_b[i], a_w_s[i], a_b_s[i])
            h = _matmul(gz, a_w_out, i, epilogue="residual", extra=h, out_dtype=F32)
        else:
            j = layer - n_a
            if layer == n_a:
                hkv, hq = _rmsnorm(h, jnp.stack([kv_norm, b_norm[j]]))
                k = _matmul(hkv, w_k, epilogue="headnorm", extra=k_norm)
                v = _matmul(hkv, w_v)
            else:
                (hq,) = _rmsnorm(h, b_norm[j:j + 1])
            q = _matmul(hq, w_q, j, epilogue="headnorm", extra=q_norm[j], scale=HEAD_DIM ** -0.5)
            lam_vecs = jnp.stack([lam_q1[j], lam_k1[j], lam_q2[j], lam_k2[j]]).astype(F32)
            o = _diff_attention(q, k, v, rel_bias, lam_vecs, subln[j], batch=batch, seq=seq,
                                lambda_init=_lambda_init(layer))
            h = _matmul(o, w_o, j, epilogue="residual", extra=h, out_dtype=F32)
        h = _hier_moe(h, m_norm[layer], m_w_group[layer], m_w_router[layer], w1, w3, w2, layer)
    return h.reshape(batch, seq, d)
```

```python
import functools
import math

import jax
import jax.numpy as jnp
import numpy as np
from jax import lax
from jax.experimental import pallas as pl
from jax.experimental.pallas import tpu as pltpu

F32 = jnp.float32
BF16 = jnp.bfloat16

CHUNK = 128
HEAD_DIM = 128
N_BUCKETS = 32
MAX_DISTANCE = 128
TOP_K = 2
EPS = 1e-6
LN_EPS = 1e-5
NEG_INF = -1e30
FINITE_MAX = 3.0e38

LANES = 128
VMEM_LIMIT_BYTES = 56 * 1024 * 1024

NORM_TM = 512
MM_TM, MM_TN = 512, 1024
GATE_TM = 512
ATTN_TQ, ATTN_TK = 1024, 512
ROUTER_TM = 512
MOE_TM = 256
COMBINE_TM = 128


def _params(semantics):
    return pltpu.CompilerParams(dimension_semantics=semantics, vmem_limit_bytes=VMEM_LIMIT_BYTES)


def _lambda_init(layer):
    return 0.8 - 0.6 * math.exp(-0.3 * layer)


def _rmsnorm_kernel(x_ref, g_ref, *o_refs):
    x = x_ref[...]
    y = x * lax.rsqrt(jnp.mean(x * x, axis=-1, keepdims=True) + EPS)
    for k, o_ref in enumerate(o_refs):
        o_ref[...] = (y * g_ref[k:k + 1, :]).astype(o_ref.dtype)


def _rmsnorm(x, gains):
    t, d = x.shape
    n = gains.shape[0]
    tm = min(NORM_TM, t)
    return pl.pallas_call(
        _rmsnorm_kernel,
        grid=(t // tm,),
        in_specs=[pl.BlockSpec((tm, d), lambda i: (i, 0)),
                  pl.BlockSpec((n, d), lambda i: (0, 0))],
        out_specs=[pl.BlockSpec((tm, d), lambda i: (i, 0))] * n,
        out_shape=[jax.ShapeDtypeStruct((t, d), BF16)] * n,
        name="rmsnorm",
        compiler_params=_params(("parallel",)),
    )(x, gains)


def _gelu_tanh(x):
    return 0.5 * x * (1.0 + jnp.tanh(math.sqrt(2.0 / math.pi) * (x + 0.044715 * (x * x * x))))


def _mm_kernel(a_ref, w_ref, *rest, epilogue, scale):
    o_ref, wb_ref = rest[-2], rest[-1]

    @pl.when(pl.program_id(1) == 0)
    def _():
        wb_ref[...] = w_ref[...].astype(BF16)

    acc = jnp.dot(a_ref[...], wb_ref[...], preferred_element_type=F32)
    if epilogue == "gelu_bias":
        o_ref[...] = _gelu_tanh(acc + rest[0][...]).astype(o_ref.dtype)
    elif epilogue == "residual":
        o_ref[...] = rest[0][...] + acc
    elif epilogue == "headnorm":
        gain = rest[0][...] * scale
        for g in range(acc.shape[1] // HEAD_DIM):
            y = acc[:, g * HEAD_DIM:(g + 1) * HEAD_DIM]
            y = y * lax.rsqrt(jnp.mean(y * y, axis=-1, keepdims=True) + EPS)
            o_ref[:, g * HEAD_DIM:(g + 1) * HEAD_DIM] = (y * gain).astype(o_ref.dtype)
    else:
        o_ref[...] = acc.astype(o_ref.dtype)


def _matmul(a, w, layer=0, *, epilogue="none", extra=None, scale=1.0, out_dtype=BF16):
    m, k = a.shape
    if w.ndim == 2:
        w = w.reshape(1, *w.shape)
    n = w.shape[2]
    tm, tn = min(MM_TM, m), min(MM_TN, n)
    in_specs = [pl.BlockSpec((tm, k), lambda j, i: (i, 0)),
                pl.BlockSpec((None, k, tn), lambda j, i: (layer, 0, j), pipeline_mode=pl.Buffered(1))]
    args = [a, w]
    if epilogue == "gelu_bias":
        in_specs.append(pl.BlockSpec((1, tn), lambda j, i: (0, j)))
        args.append(extra.reshape(1, n))
    elif epilogue == "residual":
        in_specs.append(pl.BlockSpec((tm, tn), lambda j, i: (i, j)))
        args.append(extra)
    elif epilogue == "headnorm":
        in_specs.append(pl.BlockSpec((1, HEAD_DIM), lambda j, i: (0, 0)))
        args.append(extra.reshape(1, HEAD_DIM))
    return pl.pallas_call(
        functools.partial(_mm_kernel, epilogue=epilogue, scale=scale),
        grid=(n // tn, m // tm),
        in_specs=in_specs,
        out_specs=pl.BlockSpec((tm, tn), lambda j, i: (i, j)),
        out_shape=jax.ShapeDtypeStruct((m, n), out_dtype),
        scratch_shapes=[pltpu.VMEM((k, tn), BF16)],
        name="mm_" + epilogue,
        compiler_params=_params(("parallel", "arbitrary")),
    )(*args)


def _gmlp_gate_kernel(u_ref, v_ref, lng_ref, lnb_ref, ws_ref, bs_ref, o_ref, *, n_chunks, groups):
    v = v_ref[...].astype(F32)
    mu = jnp.mean(v, axis=-1, keepdims=True)
    vc = v - mu
    var = jnp.mean(vc * vc, axis=-1, keepdims=True)
    vn = (vc * lax.rsqrt(var + LN_EPS) * lng_ref[...] + lnb_ref[...]).astype(BF16)
    gw = v.shape[1] // groups
    row = lax.broadcasted_iota(jnp.int32, (CHUNK, CHUNK), 0)
    col = lax.broadcasted_iota(jnp.int32, (CHUNK, CHUNK), 1)
    causal = row >= col
    for g in range(groups):
        w = jnp.where(causal, ws_ref[g], 0.0).astype(BF16)
        b = bs_ref[g]
        for c in range(n_chunks):
            rows = slice(c * CHUNK, (c + 1) * CHUNK)
            cols = slice(g * gw, (g + 1) * gw)
            s = jnp.dot(w, vn[rows, cols], preferred_element_type=F32) + b
            o_ref[rows, cols] = (u_ref[rows, cols].astype(F32) * s).astype(o_ref.dtype)


def _gmlp_gate(z, ln_g, ln_b, w_s, b_s):
    t, d2 = z.shape
    dg = d2 // 2
    groups = w_s.shape[0]
    tm = min(GATE_TM, t)
    return pl.pallas_call(
        functools.partial(_gmlp_gate_kernel, n_chunks=tm // CHUNK, groups=groups),
        grid=(t // tm,),
        in_specs=[pl.BlockSpec((tm, dg), lambda i: (i, 0)),
                  pl.BlockSpec((tm, dg), lambda i: (i, 1)),
                  pl.BlockSpec((1, dg), lambda i: (0, 0)),
                  pl.BlockSpec((1, dg), lambda i: (0, 0)),
                  pl.BlockSpec((groups, CHUNK, CHUNK), lambda i: (0, 0, 0)),
                  pl.BlockSpec((groups, CHUNK, 1), lambda i: (0, 0, 0))],
        out_specs=pl.BlockSpec((tm, dg), lambda i: (i, 0)),
        out_shape=jax.ShapeDtypeStruct((t, dg), BF16),
        name="gmlp_gate",
        compiler_params=_params(("parallel",)),
    )(z, z, ln_g.reshape(1, dg), ln_b.reshape(1, dg), w_s, b_s.reshape(groups, CHUNK, 1))


def _t5_causal_bucket(rel):
    n = jnp.maximum(rel, 0)
    max_exact = N_BUCKETS // 2
    nf = jnp.maximum(n, 1).astype(F32)
    large = max_exact + (jnp.log(nf / max_exact) / math.log(MAX_DISTANCE / max_exact)
                         * (N_BUCKETS - max_exact)).astype(jnp.int32)
    large = jnp.minimum(large, N_BUCKETS - 1)
    return jnp.where(n < max_exact, n, large)


def _attn_kernel(lam_ref, qt_ref, k_ref, vt_ref, z_ref, subln_ref, o_ref, acc_ref, m_ref, l_ref, bias_ref,
                 *, tq, tk, lambda_init):
    qi = pl.program_id(2)
    ratio = tq // tk
    first_near = ratio * qi - 1

    @pl.when(qi == 0)
    def _():
        bias_ref[0] = pltpu.roll(jnp.broadcast_to(z_ref[0], (tk, 2 * tq)), 0, 1, stride=1, stride_axis=0)

    def first_query(near):
        return 0 if near is None else max(0, -near * tk)

    def row_pad(x, c0, fill):
        return x if c0 == 0 else jnp.concatenate([jnp.full((1, c0), fill, F32), x], axis=1)

    def scores(j, c, near):
        ks = pl.multiple_of(j * tk, tk)
        c0 = first_query(near)
        k = k_ref[0, pl.ds(ks, tk), c * HEAD_DIM:(c + 1) * HEAD_DIM]
        qt = qt_ref[0, c * HEAD_DIM:(c + 1) * HEAD_DIM, c0:]
        s = jnp.dot(k, qt, preferred_element_type=F32)
        if near is not None:
            d = near * tk
            lo = d + tq - tk + c0
            s = s + bias_ref[0, :, lo:lo + tq - c0]
            if d <= 0:
                valid = (lax.broadcasted_iota(jnp.int32, s.shape, 1) + (c0 + d)
                         >= lax.broadcasted_iota(jnp.int32, s.shape, 0))
                s = jnp.where(valid, s, NEG_INF)
        return s

    def fixed_step(j, near):
        vt = vt_ref[0, 0, j]
        c0 = first_query(near)
        for c in range(2):
            p = jnp.exp(scores(j, c, near) - m_ref[c][:, c0:])
            l_ref[c] += row_pad(jnp.sum(p, axis=0, keepdims=True), c0, 0.0)
            acc_ref[c, :, c0:] += jnp.dot(vt, p.astype(BF16), preferred_element_type=F32)

    def online_step(j, near):
        vt = vt_ref[0, 0, j]
        c0 = first_query(near)
        for c in range(2):
            s = scores(j, c, near)
            m_old = m_ref[c]
            m_prev = m_old[:, c0:]
            m_new = jnp.maximum(m_prev, jnp.max(s, axis=0, keepdims=True))
            alpha = jnp.exp(m_prev - m_new)
            p = jnp.exp(s - m_new)
            l_ref[c] = row_pad(alpha, c0, 1.0) * l_ref[c] + row_pad(jnp.sum(p, axis=0, keepdims=True), c0, 0.0)
            acc_ref[c, :, c0:] = (alpha * acc_ref[c, :, c0:]
                                  + jnp.dot(vt, p.astype(BF16), preferred_element_type=F32))
            m_ref[c] = m_new if c0 == 0 else jnp.concatenate([m_old[:, :c0], m_new], axis=1)

    def all_blocks(step):
        def far_body(j, carry):
            step(j, None)
            return carry
        lax.fori_loop(0, jnp.maximum(first_near, 0), far_body, 0)

        @pl.when(qi >= 1)
        def _():
            step(first_near, 1)

        for r in range(ratio):
            step(first_near + 1 + r, -r)

    l_ref[...] = jnp.zeros(l_ref.shape, F32)
    acc_ref[...] = jnp.zeros(acc_ref.shape, F32)
    self_bias = bias_ref[0, 0:1, tq - tk:tq - tk + 1]
    q0 = pl.multiple_of(qi * tq, tq)
    for c in range(2):
        k_self = k_ref[0, pl.ds(q0, tq), c * HEAD_DIM:(c + 1) * HEAD_DIM].astype(F32)
        qt = qt_ref[0, c * HEAD_DIM:(c + 1) * HEAD_DIM, :].astype(F32)
        m_ref[c] = jnp.sum(qt * k_self.T, axis=0, keepdims=True) + self_bias
    all_blocks(fixed_step)

    n_bad = (jnp.sum(jnp.where(jnp.abs(acc_ref[...]) < FINITE_MAX, 0.0, 1.0))
             + jnp.sum(jnp.where(jnp.abs(l_ref[...]) < FINITE_MAX, 0.0, 1.0)))

    @pl.when(n_bad > 0.0)
    def _():
        m_ref[...] = jnp.full(m_ref.shape, NEG_INF, F32)
        l_ref[...] = jnp.zeros(l_ref.shape, F32)
        acc_ref[...] = jnp.zeros(acc_ref.shape, F32)
        all_blocks(online_step)

    lam_v = lam_ref[...]
    lam = (jnp.exp(jnp.sum(lam_v[0:1] * lam_v[1:2], axis=-1, keepdims=True))
           - jnp.exp(jnp.sum(lam_v[2:3] * lam_v[3:4], axis=-1, keepdims=True)) + lambda_init)
    ot = acc_ref[0] / l_ref[0] - lam * (acc_ref[1] / l_ref[1])
    ot = ot * lax.rsqrt(jnp.mean(ot * ot, axis=0, keepdims=True) + EPS)
    ot = ot * subln_ref[...] * (1.0 - lambda_init)
    o_ref[...] = ot.T.astype(o_ref.dtype)


def _bias_by_distance(rel_bias, tq, tk):
    table = (rel_bias - rel_bias[N_BUCKETS - 1:N_BUCKETS, :]).astype(F32)
    rel = jnp.arange(2 * tq, dtype=jnp.int32) - (tq - tk)
    onehot = (_t5_causal_bucket(rel)[:, None] == jnp.arange(N_BUCKETS, dtype=jnp.int32)[None, :]).astype(F32)
    z = jnp.dot(onehot, table, precision=lax.Precision.HIGHEST).T
    return z.reshape(z.shape[0], 1, 2 * tq)


def _diff_attention(q, k, v, rel_bias, lam_vecs, subln, *, batch, seq, lambda_init):
    t, d = q.shape
    hd2 = 2 * HEAD_DIM
    n_heads = d // hd2
    tq = min(ATTN_TQ, seq)
    tk = min(ATTN_TK, tq)
    nq, nk = seq // tq, seq // tk
    assert tk >= CHUNK and tq % tk == 0 and seq % tq == 0
    far = np.arange(tk + 1, max(seq, tk + 2), dtype=np.int64)
    far_bucket = N_BUCKETS // 2 + (np.log(far.astype(np.float32) / (N_BUCKETS // 2))
                                   / math.log(MAX_DISTANCE / (N_BUCKETS // 2))
                                   * (N_BUCKETS - N_BUCKETS // 2)).astype(np.int64)
    assert np.all(far_bucket >= N_BUCKETS - 1)
    z = _bias_by_distance(rel_bias, tq, tk)
    qt = jnp.transpose(q.reshape(batch, seq, d), (0, 2, 1))
    k3 = k.reshape(batch, seq, d)
    vt = jnp.transpose(v.reshape(batch, nk, tk, n_heads, hd2), (0, 3, 1, 4, 2))
    return pl.pallas_call(
        functools.partial(_attn_kernel, tq=tq, tk=tk, lambda_init=lambda_init),
        grid=(batch, n_heads, nq),
        in_specs=[pl.BlockSpec((4, HEAD_DIM), lambda b, h, i: (0, 0)),
                  pl.BlockSpec((1, hd2, tq), lambda b, h, i: (b, h, i)),
                  pl.BlockSpec((1, seq, hd2), lambda b, h, i: (b, 0, h)),
                  pl.BlockSpec((1, 1, nk, hd2, tk), lambda b, h, i: (b, h, 0, 0, 0)),
                  pl.BlockSpec((1, 1, 2 * tq), lambda b, h, i: (h, 0, 0)),
                  pl.BlockSpec((hd2, 1), lambda b, h, i: (0, 0))],
        out_specs=pl.BlockSpec((tq, hd2), lambda b, h, i: (b * nq + i, h)),
        out_shape=jax.ShapeDtypeStruct((t, d), BF16),
        scratch_shapes=[pltpu.VMEM((2, hd2, tq), F32),
                        pltpu.VMEM((2, 1, tq), F32),
                        pltpu.VMEM((2, 1, tq), F32),
                        pltpu.VMEM((1, tk, 2 * tq), F32)],
        name="diff_attn",
        compiler_params=_params(("parallel", "parallel", "arbitrary")),
    )(lam_vecs, qt, k3, vt, z, subln.reshape(hd2, 1))


def _router_kernel(x_ref, g_ref, wr_ref, o_ref, cnt_ref, *, n_groups, n_experts):
    @pl.when(pl.program_id(0) == 0)
    def _():
        cnt_ref[...] = jnp.zeros(cnt_ref.shape, F32)

    x = x_ref[...]
    hn = x * lax.rsqrt(jnp.mean(x * x, axis=-1, keepdims=True) + EPS) * g_ref[...]
    hi = hn.astype(BF16)
    lo = (hn - hi.astype(F32)).astype(BF16)
    r1 = jnp.dot(hi, wr_ref[...], preferred_element_type=F32)
    r2 = jnp.dot(lo, wr_ref[:, :LANES], preferred_element_type=F32)
    logits = r1[:, :LANES] + r1[:, LANES:] + r2
    lane = lax.broadcasted_iota(jnp.int32, logits.shape, 1).astype(F32)
    no_lane = float(LANES)
    is_g = lane < n_groups
    gl = jnp.where(is_g, logits, -jnp.inf)
    gmax = jnp.max(gl, axis=-1, keepdims=True)
    gidx = jnp.min(jnp.where(gl == gmax, lane, no_lane), axis=-1, keepdims=True)
    gsum = jnp.sum(jnp.where(is_g, jnp.exp(gl - gmax), 0.0), axis=-1, keepdims=True)
    g_gate = 1.0 / gsum
    lo = n_groups + gidx * n_experts
    in_grp = (lane >= lo) & (lane < lo + n_experts)
    el = jnp.where(in_grp, logits, -jnp.inf)
    t1 = jnp.max(el, axis=-1, keepdims=True)
    i1 = jnp.min(jnp.where(el == t1, lane, no_lane), axis=-1, keepdims=True)
    el2 = jnp.where(lane == i1, -jnp.inf, el)
    t2 = jnp.max(el2, axis=-1, keepdims=True)
    i2 = jnp.min(jnp.where(el2 == t2, lane, no_lane), axis=-1, keepdims=True)
    e = jnp.exp(t2 - t1)
    w1 = g_gate / (1.0 + e)
    w2 = g_gate * e / (1.0 + e)
    e1 = i1 - n_groups
    e2 = i2 - n_groups
    tm = logits.shape[0]
    oh1 = lane == e1
    oh2 = lane == e2
    strict_lower = (lax.broadcasted_iota(jnp.int32, (tm, tm), 0)
                    > lax.broadcasted_iota(jnp.int32, (tm, tm), 1)).astype(BF16)
    pre1 = jnp.dot(strict_lower, oh1.astype(BF16), preferred_element_type=F32)
    pre2 = jnp.dot(strict_lower, oh2.astype(BF16), preferred_element_type=F32)
    tot1 = jnp.sum(oh1.astype(F32), axis=0, keepdims=True)
    tot2 = jnp.sum(oh2.astype(F32), axis=0, keepdims=True)
    base = cnt_ref[...]
    rank1 = jnp.sum(jnp.where(oh1, base + pre1, 0.0), axis=-1, keepdims=True)
    rank2 = jnp.sum(jnp.where(oh2, base + tot1 + pre2, 0.0), axis=-1, keepdims=True)
    cnt_ref[...] = base + tot1 + tot2
    out = jnp.zeros(logits.shape, F32)
    for k, val in enumerate((e1, e2, w1, w2, rank1, rank2)):
        out = jnp.where(lane == k, val, out)
    o_ref[...] = out


def _router(h, gain, w_group, w_router):
    t, d = h.shape
    n_groups, _, n_experts = w_router.shape
    tm = min(ROUTER_TM, t)
    assert n_groups + n_groups * n_experts <= LANES
    wr = jnp.concatenate([w_group, jnp.transpose(w_router, (1, 0, 2)).reshape(d, n_groups * n_experts)], axis=1)
    wr = jnp.pad(wr, ((0, 0), (0, LANES - wr.shape[1])))
    wr_hi = wr.astype(BF16)
    wr = jnp.concatenate([wr_hi, (wr - wr_hi.astype(F32)).astype(BF16)], axis=1)
    return pl.pallas_call(
        functools.partial(_router_kernel, n_groups=n_groups, n_experts=n_experts),
        grid=(t // tm,),
        in_specs=[pl.BlockSpec((tm, d), lambda i: (i, 0)),
                  pl.BlockSpec((1, d), lambda i: (0, 0)),
                  pl.BlockSpec((d, 2 * LANES), lambda i: (0, 0))],
        out_specs=[pl.BlockSpec((tm, LANES), lambda i: (i, 0)),
                   pl.BlockSpec((1, LANES), lambda i: (0, 0))],
        out_shape=[jax.ShapeDtypeStruct((t, LANES), F32),
                   jax.ShapeDtypeStruct((1, LANES), F32)],
        name="moe_router",
        compiler_params=_params(("arbitrary",)),
    )(h, gain.reshape(1, d), wr)


def _row_copy(src_hbm, idx, buf, slot, r, sem):
    return pltpu.make_async_copy(src_hbm.at[pl.ds(idx, 1), :], buf.at[slot, pl.ds(r, 1), :], sem.at[slot])


def _gather_start(src_hbm, idx_ref, base, buf, slot, sem, n_rows):
    def body(r, c):
        _row_copy(src_hbm, idx_ref[base + r], buf, slot, r, sem).start()
        return c
    lax.fori_loop(0, n_rows, body, 0, unroll=8)


def _gather_wait(src_hbm, buf, slot, sem, n_rows):
    def body(r, c):
        _row_copy(src_hbm, 0, buf, slot, r, sem).wait()
        return c
    lax.fori_loop(0, n_rows, body, 0, unroll=8)


def _expert_changed(te_ref, t):
    return (t == 0) | (te_ref[t] != te_ref[jnp.maximum(t - 1, 0)])


def _moe_up_kernel(te_ref, nv_ref, tok_ref, h_hbm, g_ref, w1_ref, w3_ref, o_ref, buf, sem, w1b_ref, w3b_ref,
                   *, tm):
    t = pl.program_id(0)
    n_valid = nv_ref[0]
    slot = t % 2

    @pl.when(_expert_changed(te_ref, t))
    def _():
        w1b_ref[...] = w1_ref[0].astype(BF16)
        w3b_ref[...] = w3_ref[0].astype(BF16)

    @pl.when(t == 0)
    def _():
        _gather_start(h_hbm, tok_ref, 0, buf, 0, sem, tm)

    @pl.when(t + 1 < n_valid)
    def _():
        _gather_start(h_hbm, tok_ref, (t + 1) * tm, buf, 1 - slot, sem, tm)

    @pl.when((t < n_valid) | (t == 0))
    def _():
        _gather_wait(h_hbm, buf, slot, sem, tm)

    @pl.when(t < n_valid)
    def _():
        x = buf[slot]
        hn = (x * lax.rsqrt(jnp.mean(x * x, axis=-1, keepdims=True) + EPS) * g_ref[...]).astype(BF16)
        a = jnp.dot(hn, w1b_ref[...], preferred_element_type=F32)
        b = jnp.dot(hn, w3b_ref[...], preferred_element_type=F32)
        o_ref[...] = (a * jax.nn.sigmoid(a) * b).astype(o_ref.dtype)

    @pl.when(t >= n_valid)
    def _():
        o_ref[...] = jnp.zeros(o_ref.shape, o_ref.dtype)


def _moe_down_kernel(te_ref, nv_ref, h_ref, w2_ref, o_ref, w2b_ref):
    t = pl.program_id(0)

    @pl.when(_expert_changed(te_ref, t))
    def _():
        w2b_ref[...] = w2_ref[0].astype(BF16)

    @pl.when(t < nv_ref[0])
    def _():
        o_ref[...] = jnp.dot(h_ref[...], w2b_ref[...], preferred_element_type=F32)

    @pl.when(t >= nv_ref[0])
    def _():
        o_ref[...] = jnp.zeros(o_ref.shape, o_ref.dtype)


def _combine_kernel(pos_ref, y_hbm, h_ref, route_ref, o_ref, buf0, buf1, sem0, sem1, *, tm, n_tok):
    t = pl.program_id(0)
    n = pl.num_programs(0)
    slot = t % 2

    def start(step, s):
        _gather_start(y_hbm, pos_ref, step * tm, buf0, s, sem0, tm)
        _gather_start(y_hbm, pos_ref, n_tok + step * tm, buf1, s, sem1, tm)

    @pl.when(t == 0)
    def _():
        start(0, 0)

    @pl.when(t + 1 < n)
    def _():
        start(t + 1, 1 - slot)

    _gather_wait(y_hbm, buf0, slot, sem0, tm)
    _gather_wait(y_hbm, buf1, slot, sem1, tm)
    w = route_ref[...]
    o_ref[...] = h_ref[...] + w[:, 2:3] * buf0[slot] + w[:, 3:4] * buf1[slot]


def _hier_moe(h, gain, w_group, w_router, w1, w3, w2, layer):
    t, d = h.shape
    f = w1.shape[2]
    ne = w_router.shape[0] * w_router.shape[2]
    e0 = layer * ne
    tm, tc = MOE_TM, COMBINE_TM
    route, counts = _router(h, gain, w_group, w_router)
    e_pair = jnp.concatenate([route[:, 0], route[:, 1]]).astype(jnp.int32)
    rank = jnp.concatenate([route[:, 4], route[:, 5]]).astype(jnp.int32)
    tok_pair = jnp.tile(jnp.arange(t, dtype=jnp.int32), TOP_K)

    counts = counts[0, :ne].astype(jnp.int32)
    tiles_e = (counts + tm - 1) // tm
    tile_end = jnp.cumsum(tiles_e)
    row_start = (tile_end - tiles_e) * tm
    is_e = e_pair[:, None] == jnp.arange(ne, dtype=jnp.int32)[None, :]
    dest = jnp.sum(jnp.where(is_e, row_start[None, :], 0), axis=1) + rank
    n_tiles = (TOP_K * t) // tm + ne
    n_rows = n_tiles * tm
    tok_sorted = jnp.zeros((n_rows,), jnp.int32).at[dest].set(tok_pair)
    n_valid = tile_end[-1]
    tile_ids = jnp.minimum(jnp.arange(n_tiles, dtype=jnp.int32), n_valid - 1)
    tile_expert = jnp.sum((tile_end[None, :] <= tile_ids[:, None]).astype(jnp.int32), axis=1)
    n_valid = n_valid.reshape(1).astype(jnp.int32)

    hid = pl.pallas_call(
        functools.partial(_moe_up_kernel, tm=tm),
        grid_spec=pltpu.PrefetchScalarGridSpec(
            num_scalar_prefetch=3,
            grid=(n_tiles,),
            in_specs=[pl.BlockSpec(memory_space=pl.ANY),
                      pl.BlockSpec((1, d), lambda i, te, nv, tok: (0, 0)),
                      pl.BlockSpec((1, d, f), lambda i, te, nv, tok: (e0 + te[i], 0, 0)),
                      pl.BlockSpec((1, d, f), lambda i, te, nv, tok: (e0 + te[i], 0, 0))],
            out_specs=pl.BlockSpec((tm, f), lambda i, te, nv, tok: (i, 0)),
            scratch_shapes=[pltpu.VMEM((2, tm, d), F32), pltpu.SemaphoreType.DMA((2,)),
                            pltpu.VMEM((d, f), BF16), pltpu.VMEM((d, f), BF16)]),
        out_shape=jax.ShapeDtypeStruct((n_rows, f), BF16),
        name="moe_up",
        compiler_params=_params(("arbitrary",)),
    )(tile_expert, n_valid, tok_sorted, h, gain.reshape(1, d), w1, w3)

    ys = pl.pallas_call(
        _moe_down_kernel,
        grid_spec=pltpu.PrefetchScalarGridSpec(
            num_scalar_prefetch=2,
            grid=(n_tiles,),
            in_specs=[pl.BlockSpec((tm, f), lambda i, te, nv: (i, 0)),
                      pl.BlockSpec((1, f, d), lambda i, te, nv: (e0 + te[i], 0, 0))],
            out_specs=pl.BlockSpec((tm, d), lambda i, te, nv: (i, 0)),
            scratch_shapes=[pltpu.VMEM((f, d), BF16)]),
        out_shape=jax.ShapeDtypeStruct((n_rows, d), F32),
        name="moe_down",
        compiler_params=_params(("arbitrary",)),
    )(tile_expert, n_valid, hid, w2)

    return pl.pallas_call(
        functools.partial(_combine_kernel, tm=tc, n_tok=t),
        grid_spec=pltpu.PrefetchScalarGridSpec(
            num_scalar_prefetch=1,
            grid=(t // tc,),
            in_specs=[pl.BlockSpec(memory_space=pl.ANY),
                      pl.BlockSpec((tc, d), lambda i, pos: (i, 0)),
                      pl.BlockSpec((tc, LANES), lambda i, pos: (i, 0))],
            out_specs=pl.BlockSpec((tc, d), lambda i, pos: (i, 0)),
            scratch_shapes=[pltpu.VMEM((2, tc, d), F32), pltpu.VMEM((2, tc, d), F32),
                            pltpu.SemaphoreType.DMA((2,)), pltpu.SemaphoreType.DMA((2,))]),
        out_shape=jax.ShapeDtypeStruct((t, d), F32),
        name="moe_combine",
        compiler_params=_params(("arbitrary",)),
    )(dest.astype(jnp.int32), ys, h, route)


def kernel(x, a_norm, a_w_in, a_b_in, a_ln_g, a_ln_b, a_w_s, a_b_s, a_w_out, kv_norm, w_k, w_v, k_norm, b_norm, w_q, q_norm, lam_q1, lam_k1, lam_q2, lam_k2, subln, w_o, rel_bias, m_norm, m_w_group, m_w_router, m_w1, m_w3, m_w2):
    batch, seq, d = x.shape
    depth = m_norm.shape[0]
    n_a = a_norm.shape[0]
    h = x.reshape(batch * seq, d)
    f = m_w1.shape[-1]
    w1 = m_w1.reshape(-1, d, f)
    w3 = m_w3.reshape(-1, d, f)
    w2 = m_w2.reshape(-1, f, d)
    k = v = None
    for layer in range(depth):
        if layer < n_a:
            i = layer
            (hn,) = _rmsnorm(h, a_norm[i:i + 1])
            z = _matmul(hn, a_w_in, i, epilogue="gelu_bias", extra=a_b_in[i])
            gz = _gmlp_gate(z, a_ln_g[i], a_ln_b[i], a_w_s[i], a_b_s[i])
            h = _matmul(gz, a_w_out, i, epilogue="residual", extra=h, out_dtype=F32)
        else:
            j = layer - n_a
            if layer == n_a:
                hkv, hq = _rmsnorm(h, jnp.stack([kv_norm, b_norm[j]]))
                k = _matmul(hkv, w_k, epilogue="headnorm", extra=k_norm)
                v = _matmul(hkv, w_v)
            else:
                (hq,) = _rmsnorm(h, b_norm[j:j + 1])
            q = _matmul(hq, w_q, j, epilogue="headnorm", extra=q_norm[j], scale=HEAD_DIM ** -0.5)
            lam_vecs = jnp.stack([lam_q1[j], lam_k1[j], lam_q2[j], lam_k2[j]]).astype(F32)
            o = _diff_attention(q, k, v, rel_bias, lam_vecs, subln[j], batch=batch, seq=seq,
                                lambda_init=_lambda_init(layer))
            h = _matmul(o, w_o, j, epilogue="residual", extra=h, out_dtype=F32)
        h = _hier_moe(h, m_norm[layer], m_w_group[layer], m_w_router[layer], w1, w3, w2, layer)
    return h.reshape(batch, seq, d)
```

```python
import functools
import math

import jax
import jax.numpy as jnp
import numpy as np
from jax import lax
from jax.experimental import pallas as pl
from jax.experimental.pallas import tpu as pltpu

F32 = jnp.float32
BF16 = jnp.bfloat16

CHUNK = 128
HEAD_DIM = 128
N_BUCKETS = 32
MAX_DISTANCE = 128
TOP_K = 2
EPS = 1e-6
LN_EPS = 1e-5
NEG_INF = -1e30
FINITE_MAX = 3.0e38

LANES = 128
VMEM_LIMIT_BYTES = 56 * 1024 * 1024

NORM_TM = 512
MM_TM, MM_TN = 512, 1024
GATE_TM = 512
ATTN_TQ, ATTN_TK = 1024, 512
ROUTER_TM = 512
MOE_TM = 256
COMBINE_TM = 128


def _params(semantics):
    return pltpu.CompilerParams(dimension_semantics=semantics, vmem_limit_bytes=VMEM_LIMIT_BYTES)


def _lambda_init(layer):
    return 0.8 - 0.6 * math.exp(-0.3 * layer)


def _rmsnorm_kernel(x_ref, g_ref, *o_refs):
    x = x_ref[...]
    y = x * lax.rsqrt(jnp.mean(x * x, axis=-1, keepdims=True) + EPS)
    for k, o_ref in enumerate(o_refs):
        o_ref[...] = (y * g_ref[k:k + 1, :]).astype(o_ref.dtype)


def _rmsnorm(x, gains):
    t, d = x.shape
    n = gains.shape[0]
    tm = min(NORM_TM, t)
    return pl.pallas_call(
        _rmsnorm_kernel,
        grid=(t // tm,),
        in_specs=[pl.BlockSpec((tm, d), lambda i: (i, 0)),
                  pl.BlockSpec((n, d), lambda i: (0, 0))],
        out_specs=[pl.BlockSpec((tm, d), lambda i: (i, 0))] * n,
        out_shape=[jax.ShapeDtypeStruct((t, d), BF16)] * n,
        name="rmsnorm",
        compiler_params=_params(("parallel",)),
    )(x, gains)


def _gelu_tanh(x):
    return 0.5 * x * (1.0 + jnp.tanh(math.sqrt(2.0 / math.pi) * (x + 0.044715 * (x * x * x))))


def _mm_kernel(a_ref, w_ref, *rest, epilogue, scale):
    o_ref, wb_ref = rest[-2], rest[-1]

    @pl.when(pl.program_id(1) == 0)
    def _():
        wb_ref[...] = w_ref[...].astype(BF16)

    acc = jnp.dot(a_ref[...], wb_ref[...], preferred_element_type=F32)
    if epilogue == "gelu_bias":
        o_ref[...] = _gelu_tanh(acc + rest[0][...]).astype(o_ref.dtype)
    elif epilogue == "residual":
        o_ref[...] = rest[0][...] + acc
    elif epilogue == "headnorm":
        gain = rest[0][...] * scale
        for g in range(acc.shape[1] // HEAD_DIM):
            y = acc[:, g * HEAD_DIM:(g + 1) * HEAD_DIM]
            y = y * lax.rsqrt(jnp.mean(y * y, axis=-1, keepdims=True) + EPS)
            o_ref[:, g * HEAD_DIM:(g + 1) * HEAD_DIM] = (y * gain).astype(o_ref.dtype)
    else:
        o_ref[...] = acc.astype(o_ref.dtype)


def _matmul(a, w, layer=0, *, epilogue="none", extra=None, scale=1.0, out_dtype=BF16):
    m, k = a.shape
    if w.ndim == 2:
        w = w.reshape(1, *w.shape)
    n = w.shape[2]
    tm, tn = min(MM_TM, m), min(MM_TN, n)
    in_specs = [pl.BlockSpec((tm, k), lambda j, i: (i, 0)),
                pl.BlockSpec((None, k, tn), lambda j, i: (layer, 0, j), pipeline_mode=pl.Buffered(1))]
    args = [a, w]
    if epilogue == "gelu_bias":
        in_specs.append(pl.BlockSpec((1, tn), lambda j, i: (0, j)))
        args.append(extra.reshape(1, n))
    elif epilogue == "residual":
        in_specs.append(pl.BlockSpec((tm, tn), lambda j, i: (i, j)))
        args.append(extra)
    elif epilogue == "headnorm":
        in_specs.append(pl.BlockSpec((1, HEAD_DIM), lambda j, i: (0, 0)))
        args.append(extra.reshape(1, HEAD_DIM))
    return pl.pallas_call(
        functools.partial(_mm_kernel, epilogue=epilogue, scale=scale),
        grid=(n // tn, m // tm),
        in_specs=in_specs,
        out_specs=pl.BlockSpec((tm, tn), lambda j, i: (i, j)),
        out_shape=jax.ShapeDtypeStruct((m, n), out_dtype),
        scratch_shapes=[pltpu.VMEM((k, tn), BF16)],
        name="mm_" + epilogue,
        compiler_params=_params(("parallel", "arbitrary")),
    )(*args)


def _gmlp_gate_kernel(u_ref, v_ref, lng_ref, lnb_ref, ws_ref, bs_ref, o_ref, *, n_chunks, groups):
    v = v_ref[...].astype(F32)
    mu = jnp.mean(v, axis=-1, keepdims=True)
    vc = v - mu
    var = jnp.mean(vc * vc, axis=-1, keepdims=True)
    vn = (vc * lax.rsqrt(var + LN_EPS) * lng_ref[...] + lnb_ref[...]).astype(BF16)
    gw = v.shape[1] // groups
    row = lax.broadcasted_iota(jnp.int32, (CHUNK, CHUNK), 0)
    col = lax.broadcasted_iota(jnp.int32, (CHUNK, CHUNK), 1)
    causal = row >= col
    for g in range(groups):
        w = jnp.where(causal, ws_ref[g], 0.0).astype(BF16)
        b = bs_ref[g]
        for c in range(n_chunks):
            rows = slice(c * CHUNK, (c + 1) * CHUNK)
            cols = slice(g * gw, (g + 1) * gw)
            s = jnp.dot(w, vn[rows, cols], preferred_element_type=F32) + b
            o_ref[rows, cols] = (u_ref[rows, cols].astype(F32) * s).astype(o_ref.dtype)


def _gmlp_gate(z, ln_g, ln_b, w_s, b_s):
    t, d2 = z.shape
    dg = d2 // 2
    groups = w_s.shape[0]
    tm = min(GATE_TM, t)
    return pl.pallas_call(
        functools.partial(_gmlp_gate_kernel, n_chunks=tm // CHUNK, groups=groups),
        grid=(t // tm,),
        in_specs=[pl.BlockSpec((tm, dg), lambda i: (i, 0)),
                  pl.BlockSpec((tm, dg), lambda i: (i, 1)),
                  pl.BlockSpec((1, dg), lambda i: (0, 0)),
                  pl.BlockSpec((1, dg), lambda i: (0, 0)),
                  pl.BlockSpec((groups, CHUNK, CHUNK), lambda i: (0, 0, 0)),
                  pl.BlockSpec((groups, CHUNK, 1), lambda i: (0, 0, 0))],
        out_specs=pl.BlockSpec((tm, dg), lambda i: (i, 0)),
        out_shape=jax.ShapeDtypeStruct((t, dg), BF16),
        name="gmlp_gate",
        compiler_params=_params(("parallel",)),
    )(z, z, ln_g.reshape(1, dg), ln_b.reshape(1, dg), w_s, b_s.reshape(groups, CHUNK, 1))


def _t5_causal_bucket(rel):
    n = jnp.maximum(rel, 0)
    max_exact = N_BUCKETS // 2
    nf = jnp.maximum(n, 1).astype(F32)
    large = max_exact + (jnp.log(nf / max_exact) / math.log(MAX_DISTANCE / max_exact)
                         * (N_BUCKETS - max_exact)).astype(jnp.int32)
    large = jnp.minimum(large, N_BUCKETS - 1)
    return jnp.where(n < max_exact, n, large)


def _attn_kernel(lam_ref, qt_ref, k_ref, vt_ref, z_ref, subln_ref, o_ref, acc_ref, m_ref, l_ref, bias_ref,
                 *, tq, tk, lambda_init):
    qi = pl.program_id(2)
    ratio = tq // tk
    first_near = ratio * qi - 1

    @pl.when(qi == 0)
    def _():
        bias_ref[0] = pltpu.roll(jnp.broadcast_to(z_ref[0], (tk, 2 * tq)), 0, 1, stride=1, stride_axis=0)

    def first_query(near):
        return 0 if near is None else max(0, -near * tk)

    def row_pad(x, c0, fill):
        return x if c0 == 0 else jnp.concatenate([jnp.full((1, c0), fill, F32), x], axis=1)

    def scores(j, c, near):
        ks = pl.multiple_of(j * tk, tk)
        c0 = first_query(near)
        k = k_ref[0, pl.ds(ks, tk), c * HEAD_DIM:(c + 1) * HEAD_DIM]
        qt = qt_ref[0, c * HEAD_DIM:(c + 1) * HEAD_DIM, c0:]
        s = jnp.dot(k, qt, preferred_element_type=F32)
        if near is not None:
            d = near * tk
            lo = d + tq - tk + c0
            s = s + bias_ref[0, :, lo:lo + tq - c0]
            if d <= 0:
                valid = (lax.broadcasted_iota(jnp.int32, s.shape, 1) + (c0 + d)
                         >= lax.broadcasted_iota(jnp.int32, s.shape, 0))
                s = jnp.where(valid, s, NEG_INF)
        return s

    def fixed_step(j, near):
        vt = vt_ref[0, 0, j]
        c0 = first_query(near)
        for c in range(2):
            p = jnp.exp(scores(j, c, near) - m_ref[c][:, c0:])
            l_ref[c] += row_pad(jnp.sum(p, axis=0, keepdims=True), c0, 0.0)
            acc_ref[c, :, c0:] += jnp.dot(vt, p.astype(BF16), preferred_element_type=F32)

    def online_step(j, near):
        vt = vt_ref[0, 0, j]
        c0 = first_query(near)
        for c in range(2):
            s = scores(j, c, near)
            m_old = m_ref[c]
            m_prev = m_old[:, c0:]
            m_new = jnp.maximum(m_prev, jnp.max(s, axis=0, keepdims=True))
            alpha = jnp.exp(m_prev - m_new)
            p = jnp.exp(s - m_new)
            l_ref[c] = row_pad(alpha, c0, 1.0) * l_ref[c] + row_pad(jnp.sum(p, axis=0, keepdims=True), c0, 0.0)
            acc_ref[c, :, c0:] = (alpha * acc_ref[c, :, c0:]
                                  + jnp.dot(vt, p.astype(BF16), preferred_element_type=F32))
            m_ref[c] = m_new if c0 == 0 else jnp.concatenate([m_old[:, :c0], m_new], axis=1)

    def all_blocks(step):
        def far_body(j, carry):
            step(j, None)
            return carry
        lax.fori_loop(0, jnp.maximum(first_near, 0), far_body, 0)

        @pl.when(qi >= 1)
        def _():
            step(first_near, 1)

        for r in range(ratio):
            step(first_near + 1 + r, -r)

    l_ref[...] = jnp.zeros(l_ref.shape, F32)
    acc_ref[...] = jnp.zeros(acc_ref.shape, F32)
    self_bias = bias_ref[0, 0:1, tq - tk:tq - tk + 1]
    q0 = pl.multiple_of(qi * tq, tq)
    for c in range(2):
        k_self = k_ref[0, pl.ds(q0, tq), c * HEAD_DIM:(c + 1) * HEAD_DIM].astype(F32)
        qt = qt_ref[0, c * HEAD_DIM:(c + 1) * HEAD_DIM, :].astype(F32)
        m_ref[c] = jnp.sum(qt * k_self.T, axis=0, keepdims=True) + self_bias
    all_blocks(fixed_step)

    n_bad = (jnp.sum(jnp.where(jnp.abs(acc_ref[...]) < FINITE_MAX, 0.0, 1.0))
             + jnp.sum(jnp.where(jnp.abs(l_ref[...]) < FINITE_MAX, 0.0, 1.0)))

    @pl.when(n_bad > 0.0)
    def _():
        m_ref[...] = jnp.full(m_ref.shape, NEG_INF, F32)
        l_ref[...] = jnp.zeros(l_ref.shape, F32)
        acc_ref[...] = jnp.zeros(acc_ref.shape, F32)
        all_blocks(online_step)

    lam_v = lam_ref[...]
    lam = (jnp.exp(jnp.sum(lam_v[0:1] * lam_v[1:2], axis=-1, keepdims=True))
           - jnp.exp(jnp.sum(lam_v[2:3] * lam_v[3:4], axis=-1, keepdims=True)) + lambda_init)
    ot = acc_ref[0] / l_ref[0] - lam * (acc_ref[1] / l_ref[1])
    ot = ot * lax.rsqrt(jnp.mean(ot * ot, axis=0, keepdims=True) + EPS)
    ot = ot * subln_ref[...] * (1.0 - lambda_init)
    o_ref[...] = ot.T.astype(o_ref.dtype)


def _bias_by_distance(rel_bias, tq, tk):
    table = (rel_bias - rel_bias[N_BUCKETS - 1:N_BUCKETS, :]).astype(F32)
    rel = jnp.arange(2 * tq, dtype=jnp.int32) - (tq - tk)
    onehot = (_t5_causal_bucket(rel)[:, None] == jnp.arange(N_BUCKETS, dtype=jnp.int32)[None, :]).astype(F32)
    z = jnp.dot(onehot, table, precision=lax.Precision.HIGHEST).T
    return z.reshape(z.shape[0], 1, 2 * tq)


def _diff_attention(q, k, v, rel_bias, lam_vecs, subln, *, batch, seq, lambda_init):
    t, d = q.shape
    hd2 = 2 * HEAD_DIM
    n_heads = d // hd2
    tq = min(ATTN_TQ, seq)
    tk = min(ATTN_TK, tq)
    nq, nk = seq // tq, seq // tk
    assert tk >= CHUNK and tq % tk == 0 and seq % tq == 0
    far = np.arange(tk + 1, max(seq, tk + 2), dtype=np.int64)
    far_bucket = N_BUCKETS // 2 + (np.log(far.astype(np.float32) / (N_BUCKETS // 2))
                                   / math.log(MAX_DISTANCE / (N_BUCKETS // 2))
                                   * (N_BUCKETS - N_BUCKETS // 2)).astype(np.int64)
    assert np.all(far_bucket >= N_BUCKETS - 1)
    z = _bias_by_distance(rel_bias, tq, tk)
    qt = jnp.transpose(q.reshape(batch, seq, d), (0, 2, 1))
    k3 = k.reshape(batch, seq, d)
    vt = jnp.transpose(v.reshape(batch, nk, tk, n_heads, hd2), (0, 3, 1, 4, 2))
    return pl.pallas_call(
        functools.partial(_attn_kernel, tq=tq, tk=tk, lambda_init=lambda_init),
        grid=(batch, n_heads, nq),
        in_specs=[pl.BlockSpec((4, HEAD_DIM), lambda b, h, i: (0, 0)),
                  pl.BlockSpec((1, hd2, tq), lambda b, h, i: (b, h, i)),
                  pl.BlockSpec((1, seq, hd2), lambda b, h, i: (b, 0, h)),
                  pl.BlockSpec((1, 1, nk, hd2, tk), lambda b, h, i: (b, h, 0, 0, 0)),
                  pl.BlockSpec((1, 1, 2 * tq), lambda b, h, i: (h, 0, 0)),
                  pl.BlockSpec((hd2, 1), lambda b, h, i: (0, 0))],
        out_specs=pl.BlockSpec((tq, hd2), lambda b, h, i: (b * nq + i, h)),
        out_shape=jax.ShapeDtypeStruct((t, d), BF16),
        scratch_shapes=[pltpu.VMEM((2, hd2, tq), F32),
                        pltpu.VMEM((2, 1, tq), F32),
                        pltpu.VMEM((2, 1, tq), F32),
                        pltpu.VMEM((1, tk, 2 * tq), F32)],
        name="diff_attn",
        compiler_params=_params(("parallel", "parallel", "arbitrary")),
    )(lam_vecs, qt, k3, vt, z, subln.reshape(hd2, 1))


def _router_kernel(x_ref, g_ref, wr_ref, o_ref, cnt_ref, *, n_groups, n_experts):
    @pl.when(pl.program_id(0) == 0)
    def _():
        cnt_ref[...] = jnp.zeros(cnt_ref.shape, F32)

    x = x_ref[...]
    hn = x * lax.rsqrt(jnp.mean(x * x, axis=-1, keepdims=True) + EPS) * g_ref[...]
    hi = hn.astype(BF16)
    lo = (hn - hi.astype(F32)).astype(BF16)
    r1 = jnp.dot(hi, wr_ref[...], preferred_element_type=F32)
    r2 = jnp.dot(lo, wr_ref[:, :LANES], preferred_element_type=F32)
    logits = r1[:, :LANES] + r1[:, LANES:] + r2
    lane = lax.broadcasted_iota(jnp.int32, logits.shape, 1).astype(F32)
    no_lane = float(LANES)
    is_g = lane < n_groups
    gl = jnp.where(is_g, logits, -jnp.inf)
    gmax = jnp.max(gl, axis=-1, keepdims=True)
    gidx = jnp.min(jnp.where(gl == gmax, lane, no_lane), axis=-1, keepdims=True)
    gsum = jnp.sum(jnp.where(is_g, jnp.exp(gl - gmax), 0.0), axis=-1, keepdims=True)
    g_gate = 1.0 / gsum
    lo = n_groups + gidx * n_experts
    in_grp = (lane >= lo) & (lane < lo + n_experts)
    el = jnp.where(in_grp, logits, -jnp.inf)
    t1 = jnp.max(el, axis=-1, keepdims=True)
    i1 = jnp.min(jnp.where(el == t1, lane, no_lane), axis=-1, keepdims=True)
    el2 = jnp.where(lane == i1, -jnp.inf, el)
    t2 = jnp.max(el2, axis=-1, keepdims=True)
    i2 = jnp.min(jnp.where(el2 == t2, lane, no_lane), axis=-1, keepdims=True)
    e = jnp.exp(t2 - t1)
    w1 = g_gate / (1.0 + e)
    w2 = g_gate * e / (1.0 + e)
    e1 = i1 - n_groups
    e2 = i2 - n_groups
    tm = logits.shape[0]
    oh1 = lane == e1
    oh2 = lane == e2
    strict_lower = (lax.broadcasted_iota(jnp.int32, (tm, tm), 0)
                    > lax.broadcasted_iota(jnp.int32, (tm, tm), 1)).astype(BF16)
    pre1 = jnp.dot(strict_lower, oh1.astype(BF16), preferred_element_type=F32)
    pre2 = jnp.dot(strict_lower, oh2.astype(BF16), preferred_element_type=F32)
    tot1 = jnp.sum(oh1.astype(F32), axis=0, keepdims=True)
    tot2 = jnp.sum(oh2.astype(F32), axis=0, keepdims=True)
    base = cnt_ref[...]
    rank1 = jnp.sum(jnp.where(oh1, base + pre1, 0.0), axis=-1, keepdims=True)
    rank2 = jnp.sum(jnp.where(oh2, base + tot1 + pre2, 0.0), axis=-1, keepdims=True)
    cnt_ref[...] = base + tot1 + tot2
    out = jnp.zeros(logits.shape, F32)
    for k, val in enumerate((e1, e2, w1, w2, rank1, rank2)):
        out = jnp.where(lane == k, val, out)
    o_ref[...] = out


def _router(h, gain, w_group, w_router):
    t, d = h.shape
    n_groups, _, n_experts = w_router.shape
    tm = min(ROUTER_TM, t)
    assert n_groups + n_groups * n_experts <= LANES
    wr = jnp.concatenate([w_group, jnp.transpose(w_router, (1, 0, 2)).reshape(d, n_groups * n_experts)], axis=1)
    wr = jnp.pad(wr, ((0, 0), (0, LANES - wr.shape[1])))
    wr_hi = wr.astype(BF16)
    wr = jnp.concatenate([wr_hi, (wr - wr_hi.astype(F32)).astype(BF16)], axis=1)
    return pl.pallas_call(
        functools.partial(_router_kernel, n_groups=n_groups, n_experts=n_experts),
        grid=(t // tm,),
        in_specs=[pl.BlockSpec((tm, d), lambda i: (i, 0)),
                  pl.BlockSpec((1, d), lambda i: (0, 0)),
                  pl.BlockSpec((d, 2 * LANES), lambda i: (0, 0))],
        out_specs=[pl.BlockSpec((tm, LANES), lambda i: (i, 0)),
                   pl.BlockSpec((1, LANES), lambda i: (0, 0))],
        out_shape=[jax.ShapeDtypeStruct((t, LANES), F32),
                   jax.ShapeDtypeStruct((1, LANES), F32)],
        name="moe_router",
        compiler_params=_params(("arbitrary",)),
    )(h, gain.reshape(1, d), wr)


def _row_copy(src_hbm, idx, buf, slot, r, sem):
    return pltpu.make_async_copy(src_hbm.at[pl.ds(idx, 1), :], buf.at[slot, pl.ds(r, 1), :], sem.at[slot])


def _gather_start(src_hbm, idx_ref, base, buf, slot, sem, n_rows):
    def body(r, c):
        _row_copy(src_hbm, idx_ref[base + r], buf, slot, r, sem).start()
        return c
    lax.fori_loop(0, n_rows, body, 0, unroll=8)


def _gather_wait(src_hbm, buf, slot, sem, n_rows):
    def body(r, c):
        _row_copy(src_hbm, 0, buf, slot, r, sem).wait()
        return c
    lax.fori_loop(0, n_rows, body, 0, unroll=8)


def _expert_changed(te_ref, t):
    return (t == 0) | (te_ref[t] != te_ref[jnp.maximum(t - 1, 0)])


def _moe_up_kernel(te_ref, nv_ref, tok_ref, h_hbm, g_ref, w1_ref, w3_ref, o_ref, buf, sem, w1b_ref, w3b_ref,
                   *, tm):
    t = pl.program_id(0)
    n_valid = nv_ref[0]

    @pl.when(_expert_changed(te_ref, t))
    def _():
        w1b_ref[...] = w1_ref[0].astype(BF16)
        w3b_ref[...] = w3_ref[0].astype(BF16)

    @pl.when(t == 0)
    def _():
        _gather_start(h_hbm, tok_ref, 0, buf, 0, sem, tm)

    def step(slot):
        _gather_wait(h_hbm, buf, slot, sem, tm)
        base = jnp.minimum(t + 1, n_valid - 1) * tm
        for r in range(tm):
            _row_copy(h_hbm, tok_ref[base + r], buf, 1 - slot, r, sem).start()
        x = buf[slot]
        hn = (x * lax.rsqrt(jnp.mean(x * x, axis=-1, keepdims=True) + EPS) * g_ref[...]).astype(BF16)
        a = jnp.dot(hn, w1b_ref[...], preferred_element_type=F32)
        b = jnp.dot(hn, w3b_ref[...], preferred_element_type=F32)
        o_ref[...] = (a * jax.nn.sigmoid(a) * b).astype(o_ref.dtype)

        @pl.when(t == n_valid - 1)
        def _():
            _gather_wait(h_hbm, buf, 1 - slot, sem, tm)

    for slot in range(2):
        @pl.when((t < n_valid) & (t % 2 == slot))
        def _():
            step(slot)

    @pl.when(t >= n_valid)
    def _():
        o_ref[...] = jnp.zeros(o_ref.shape, o_ref.dtype)


def _moe_down_kernel(te_ref, nv_ref, h_ref, w2_ref, o_ref, w2b_ref):
    t = pl.program_id(0)

    @pl.when(_expert_changed(te_ref, t))
    def _():
        w2b_ref[...] = w2_ref[0].astype(BF16)

    @pl.when(t < nv_ref[0])
    def _():
        o_ref[...] = jnp.dot(h_ref[...], w2b_ref[...], preferred_element_type=F32)

    @pl.when(t >= nv_ref[0])
    def _():
        o_ref[...] = jnp.zeros(o_ref.shape, o_ref.dtype)


def _combine_kernel(pos_ref, y_hbm, h_ref, route_ref, o_ref, buf0, buf1, sem0, sem1, *, tm, n_tok):
    t = pl.program_id(0)
    n = pl.num_programs(0)
    slot = t % 2

    def start(step, s):
        _gather_start(y_hbm, pos_ref, step * tm, buf0, s, sem0, tm)
        _gather_start(y_hbm, pos_ref, n_tok + step * tm, buf1, s, sem1, tm)

    @pl.when(t == 0)
    def _():
        start(0, 0)

    @pl.when(t + 1 < n)
    def _():
        start(t + 1, 1 - slot)

    _gather_wait(y_hbm, buf0, slot, sem0, tm)
    _gather_wait(y_hbm, buf1, slot, sem1, tm)
    w = route_ref[...]
    o_ref[...] = h_ref[...] + w[:, 2:3] * buf0[slot] + w[:, 3:4] * buf1[slot]


def _hier_moe(h, gain, w_group, w_router, w1, w3, w2, layer):
    t, d = h.shape
    f = w1.shape[2]
    ne = w_router.shape[0] * w_router.shape[2]
    e0 = layer * ne
    tm, tc = MOE_TM, COMBINE_TM
    route, counts = _router(h, gain, w_group, w_router)
    e_pair = jnp.concatenate([route[:, 0], route[:, 1]]).astype(jnp.int32)
    rank = jnp.concatenate([route[:, 4], route[:, 5]]).astype(jnp.int32)
    tok_pair = jnp.tile(jnp.arange(t, dtype=jnp.int32), TOP_K)

    counts = counts[0, :ne].astype(jnp.int32)
    tiles_e = (counts + tm - 1) // tm
    tile_end = jnp.cumsum(tiles_e)
    row_start = (tile_end - tiles_e) * tm
    is_e = e_pair[:, None] == jnp.arange(ne, dtype=jnp.int32)[None, :]
    dest = jnp.sum(jnp.where(is_e, row_start[None, :], 0), axis=1) + rank
    n_tiles = (TOP_K * t) // tm + ne
    n_rows = n_tiles * tm
    tok_sorted = jnp.zeros((n_rows,), jnp.int32).at[dest].set(tok_pair)
    n_valid = tile_end[-1]
    tile_ids = jnp.minimum(jnp.arange(n_tiles, dtype=jnp.int32), n_valid - 1)
    tile_expert = jnp.sum((tile_end[None, :] <= tile_ids[:, None]).astype(jnp.int32), axis=1)
    n_valid = n_valid.reshape(1).astype(jnp.int32)

    hid = pl.pallas_call(
        functools.partial(_moe_up_kernel, tm=tm),
        grid_spec=pltpu.PrefetchScalarGridSpec(
            num_scalar_prefetch=3,
            grid=(n_tiles,),
            in_specs=[pl.BlockSpec(memory_space=pl.ANY),
                      pl.BlockSpec((1, d), lambda i, te, nv, tok: (0, 0)),
                      pl.BlockSpec((1, d, f), lambda i, te, nv, tok: (e0 + te[i], 0, 0)),
                      pl.BlockSpec((1, d, f), lambda i, te, nv, tok: (e0 + te[i], 0, 0))],
            out_specs=pl.BlockSpec((tm, f), lambda i, te, nv, tok: (i, 0)),
            scratch_shapes=[pltpu.VMEM((2, tm, d), F32), pltpu.SemaphoreType.DMA((2,)),
                            pltpu.VMEM((d, f), BF16), pltpu.VMEM((d, f), BF16)]),
        out_shape=jax.ShapeDtypeStruct((n_rows, f), BF16),
        name="moe_up",
        compiler_params=_params(("arbitrary",)),
    )(tile_expert, n_valid, tok_sorted, h, gain.reshape(1, d), w1, w3)

    ys = pl.pallas_call(
        _moe_down_kernel,
        grid_spec=pltpu.PrefetchScalarGridSpec(
            num_scalar_prefetch=2,
            grid=(n_tiles,),
            in_specs=[pl.BlockSpec((tm, f), lambda i, te, nv: (i, 0)),
                      pl.BlockSpec((1, f, d), lambda i, te, nv: (e0 + te[i], 0, 0))],
            out_specs=pl.BlockSpec((tm, d), lambda i, te, nv: (i, 0)),
            scratch_shapes=[pltpu.VMEM((f, d), BF16)]),
        out_shape=jax.ShapeDtypeStruct((n_rows, d), F32),
        name="moe_down",
        compiler_params=_params(("arbitrary",)),
    )(tile_expert, n_valid, hid, w2)

    return pl.pallas_call(
        functools.partial(_combine_kernel, tm=tc, n_tok=t),
        grid_spec=pltpu.PrefetchScalarGridSpec(
            num_scalar_prefetch=1,
            grid=(t // tc,),
            in_specs=[pl.BlockSpec(memory_space=pl.ANY),
                      pl.BlockSpec((tc, d), lambda i, pos: (i, 0)),
                      pl.BlockSpec((tc, LANES), lambda i, pos: (i, 0))],
            out_specs=pl.BlockSpec((tc, d), lambda i, pos: (i, 0)),
            scratch_shapes=[pltpu.VMEM((2, tc, d), F32), pltpu.VMEM((2, tc, d), F32),
                            pltpu.SemaphoreType.DMA((2,)), pltpu.SemaphoreType.DMA((2,))]),
        out_shape=jax.ShapeDtypeStruct((t, d), F32),
        name="moe_combine",
        compiler_params=_params(("arbitrary",)),
    )(dest.astype(jnp.int32), ys, h, route)


def kernel(x, a_norm, a_w_in, a_b_in, a_ln_g, a_ln_b, a_w_s, a_b_s, a_w_out, kv_norm, w_k, w_v, k_norm, b_norm, w_q, q_norm, lam_q1, lam_k1, lam_q2, lam_k2, subln, w_o, rel_bias, m_norm, m_w_group, m_w_router, m_w1, m_w3, m_w2):
    batch, seq, d = x.shape
    depth = m_norm.shape[0]
    n_a = a_norm.shape[0]
    h = x.reshape(batch * seq, d)
    f = m_w1.shape[-1]
    w1 = m_w1.reshape(-1, d, f)
    w3 = m_w3.reshape(-1, d, f)
    w2 = m_w2.reshape(-1, f, d)
    k = v = None
    for layer in range(depth):
        if layer < n_a:
            i = layer
            (hn,) = _rmsnorm(h, a_norm[i:i + 1])
            z = _matmul(hn, a_w_in, i, epilogue="gelu_bias", extra=a_b_in[i])
            gz = _gmlp_gate(z, a_ln_g[i], a_ln_b[i], a_w_s[i], a_b_s[i])
            h = _matmul(gz, a_w_out, i, epilogue="residual", extra=h, out_dtype=F32)
        else:
            j = layer - n_a
            if layer == n_a:
                hkv, hq = _rmsnorm(h, jnp.stack([kv_norm, b_norm[j]]))
                k = _matmul(hkv, w_k, epilogue="headnorm", extra=k_norm)
                v = _matmul(hkv, w_v)
            else:
                (hq,) = _rmsnorm(h, b_norm[j:j + 1])
            q = _matmul(hq, w_q, j, epilogue="headnorm", extra=q_norm[j], scale=HEAD_DIM ** -0.5)
            lam_vecs = jnp.stack([lam_q1[j], lam_k1[j], lam_q2[j], lam_k2[j]]).astype(F32)
            o = _diff_attention(q, k, v, rel_bias, lam_vecs, subln[j], batch=batch, seq=seq,
                                lambda_init=_lambda_init(layer))
            h = _matmul(o, w_o, j, epilogue="residual", extra=h, out_dtype=F32)
        h = _hier_moe(h, m_norm[layer], m_w_group[layer], m_w_router[layer], w1, w3, w2, layer)
    return h.reshape(batch, seq, d)
```

```python
import functools
import math

import jax
import jax.numpy as jnp
import numpy as np
from jax import lax
from jax.experimental import pallas as pl
from jax.experimental.pallas import tpu as pltpu

F32 = jnp.float32
BF16 = jnp.bfloat16

CHUNK = 128
HEAD_DIM = 128
N_BUCKETS = 32
MAX_DISTANCE = 128
TOP_K = 2
EPS = 1e-6
LN_EPS = 1e-5
NEG_INF = -1e30
FINITE_MAX = 3.0e38

LANES = 128
VMEM_LIMIT_BYTES = 56 * 1024 * 1024

NORM_TM = 512
MM_TM, MM_TM_NARROW_OUT, MM_TN = 512, 1024, 1024
GATE_TM = 512
ATTN_TQ, ATTN_TK = 1024, 512
ROUTER_TM = 512
MOE_TM = 256
COMBINE_TM = 128


def _params(semantics):
    return pltpu.CompilerParams(dimension_semantics=semantics, vmem_limit_bytes=VMEM_LIMIT_BYTES)


def _lambda_init(layer):
    return 0.8 - 0.6 * math.exp(-0.3 * layer)


def _rmsnorm_kernel(x_ref, g_ref, *o_refs):
    x = x_ref[...]
    y = x * lax.rsqrt(jnp.mean(x * x, axis=-1, keepdims=True) + EPS)
    for k, o_ref in enumerate(o_refs):
        o_ref[...] = (y * g_ref[k:k + 1, :]).astype(o_ref.dtype)


def _rmsnorm(x, gains):
    t, d = x.shape
    n = gains.shape[0]
    tm = min(NORM_TM, t)
    return pl.pallas_call(
        _rmsnorm_kernel,
        grid=(t // tm,),
        in_specs=[pl.BlockSpec((tm, d), lambda i: (i, 0)),
                  pl.BlockSpec((n, d), lambda i: (0, 0))],
        out_specs=[pl.BlockSpec((tm, d), lambda i: (i, 0))] * n,
        out_shape=[jax.ShapeDtypeStruct((t, d), BF16)] * n,
        name="rmsnorm",
        compiler_params=_params(("parallel",)),
    )(x, gains)


def _gelu_tanh(x):
    return 0.5 * x * (1.0 + jnp.tanh(math.sqrt(2.0 / math.pi) * (x + 0.044715 * (x * x * x))))


def _mm_kernel(a_ref, w_ref, *rest, epilogue, scale):
    wb_ref = rest[-1]
    o_ref = rest[1] if epilogue == "residual" else rest[-2]

    @pl.when(pl.program_id(1) == 0)
    def _():
        wb_ref[...] = w_ref[...].astype(BF16)

    acc = jnp.dot(a_ref[...], wb_ref[...], preferred_element_type=F32)
    if epilogue == "gelu_bias":
        o_ref[...] = _gelu_tanh(acc + rest[0][...]).astype(o_ref.dtype)
    elif epilogue == "residual":
        y = rest[0][...] + acc
        o_ref[...] = y
        rest[2][...] = y
    elif epilogue == "headnorm":
        gain = rest[0][...] * scale
        for g in range(acc.shape[1] // HEAD_DIM):
            y = acc[:, g * HEAD_DIM:(g + 1) * HEAD_DIM]
            y = y * lax.rsqrt(jnp.mean(y * y, axis=-1, keepdims=True) + EPS)
            o_ref[:, g * HEAD_DIM:(g + 1) * HEAD_DIM] = (y * gain).astype(o_ref.dtype)
    else:
        o_ref[...] = acc.astype(o_ref.dtype)


def _matmul(a, w, layer=0, *, epilogue="none", extra=None, scale=1.0, out_dtype=BF16):
    m, k = a.shape
    if w.ndim == 2:
        w = w.reshape(1, *w.shape)
    n = w.shape[2]
    tm, tn = min(MM_TM if out_dtype == F32 else MM_TM_NARROW_OUT, m), min(MM_TN, n)
    in_specs = [pl.BlockSpec((tm, k), lambda j, i: (i, 0)),
                pl.BlockSpec((None, k, tn), lambda j, i: (layer, 0, j), pipeline_mode=pl.Buffered(1))]
    args = [a, w]
    if epilogue == "gelu_bias":
        in_specs.append(pl.BlockSpec((1, tn), lambda j, i: (0, j)))
        args.append(extra.reshape(1, n))
    elif epilogue == "residual":
        in_specs.append(pl.BlockSpec((tm, tn), lambda j, i: (i, j)))
        args.append(extra)
    elif epilogue == "headnorm":
        in_specs.append(pl.BlockSpec((1, HEAD_DIM), lambda j, i: (0, 0)))
        args.append(extra.reshape(1, HEAD_DIM))
    out_specs = pl.BlockSpec((tm, tn), lambda j, i: (i, j))
    out_shape = jax.ShapeDtypeStruct((m, n), out_dtype)
    if epilogue == "residual":
        out_specs = [out_specs, pl.BlockSpec((tm, None, tn), lambda j, i: (i, 0, j))]
        out_shape = [out_shape, jax.ShapeDtypeStruct((m, 1, n), out_dtype)]
    return pl.pallas_call(
        functools.partial(_mm_kernel, epilogue=epilogue, scale=scale),
        grid=(n // tn, m // tm),
        in_specs=in_specs,
        out_specs=out_specs,
        out_shape=out_shape,
        scratch_shapes=[pltpu.VMEM((k, tn), BF16)],
        name="mm_" + epilogue,
        compiler_params=_params(("parallel", "arbitrary")),
    )(*args)


def _gmlp_gate_kernel(u_ref, v_ref, lng_ref, lnb_ref, ws_ref, bs_ref, o_ref, *, n_chunks, groups):
    v = v_ref[...].astype(F32)
    mu = jnp.mean(v, axis=-1, keepdims=True)
    vc = v - mu
    var = jnp.mean(vc * vc, axis=-1, keepdims=True)
    vn = (vc * lax.rsqrt(var + LN_EPS) * lng_ref[...] + lnb_ref[...]).astype(BF16)
    gw = v.shape[1] // groups
    row = lax.broadcasted_iota(jnp.int32, (CHUNK, CHUNK), 0)
    col = lax.broadcasted_iota(jnp.int32, (CHUNK, CHUNK), 1)
    causal = row >= col
    for g in range(groups):
        w = jnp.where(causal, ws_ref[g], 0.0).astype(BF16)
        b = bs_ref[g]
        for c in range(n_chunks):
            rows = slice(c * CHUNK, (c + 1) * CHUNK)
            cols = slice(g * gw, (g + 1) * gw)
            s = jnp.dot(w, vn[rows, cols], preferred_element_type=F32) + b
            o_ref[rows, cols] = (u_ref[rows, cols].astype(F32) * s).astype(o_ref.dtype)


def _gmlp_gate(z, ln_g, ln_b, w_s, b_s):
    t, d2 = z.shape
    dg = d2 // 2
    groups = w_s.shape[0]
    tm = min(GATE_TM, t)
    return pl.pallas_call(
        functools.partial(_gmlp_gate_kernel, n_chunks=tm // CHUNK, groups=groups),
        grid=(t // tm,),
        in_specs=[pl.BlockSpec((tm, dg), lambda i: (i, 0)),
                  pl.BlockSpec((tm, dg), lambda i: (i, 1)),
                  pl.BlockSpec((1, dg), lambda i: (0, 0)),
                  pl.BlockSpec((1, dg), lambda i: (0, 0)),
                  pl.BlockSpec((groups, CHUNK, CHUNK), lambda i: (0, 0, 0)),
                  pl.BlockSpec((groups, CHUNK, 1), lambda i: (0, 0, 0))],
        out_specs=pl.BlockSpec((tm, dg), lambda i: (i, 0)),
        out_shape=jax.ShapeDtypeStruct((t, dg), BF16),
        name="gmlp_gate",
        compiler_params=_params(("parallel",)),
    )(z, z, ln_g.reshape(1, dg), ln_b.reshape(1, dg), w_s, b_s.reshape(groups, CHUNK, 1))


def _t5_causal_bucket(rel):
    n = jnp.maximum(rel, 0)
    max_exact = N_BUCKETS // 2
    nf = jnp.maximum(n, 1).astype(F32)
    large = max_exact + (jnp.log(nf / max_exact) / math.log(MAX_DISTANCE / max_exact)
                         * (N_BUCKETS - max_exact)).astype(jnp.int32)
    large = jnp.minimum(large, N_BUCKETS - 1)
    return jnp.where(n < max_exact, n, large)


def _attn_kernel(lam_ref, qt_ref, k_ref, vt_ref, z_ref, subln_ref, o_ref, acc_ref, m_ref, l_ref, bias_ref,
                 *, tq, tk, lambda_init):
    qi = pl.program_id(2)
    ratio = tq // tk
    first_near = ratio * qi - 1

    @pl.when(qi == 0)
    def _():
        bias_ref[0] = pltpu.roll(jnp.broadcast_to(z_ref[0], (tk, 2 * tq)), 0, 1, stride=1, stride_axis=0)

    def first_query(near):
        return 0 if near is None else max(0, -near * tk)

    def row_pad(x, c0, fill):
        return x if c0 == 0 else jnp.concatenate([jnp.full((1, c0), fill, F32), x], axis=1)

    def scores(j, c, near):
        ks = pl.multiple_of(j * tk, tk)
        c0 = first_query(near)
        k = k_ref[0, pl.ds(ks, tk), c * HEAD_DIM:(c + 1) * HEAD_DIM]
        qt = qt_ref[0, c * HEAD_DIM:(c + 1) * HEAD_DIM, c0:]
        s = jnp.dot(k, qt, preferred_element_type=F32)
        if near is not None:
            d = near * tk
            lo = d + tq - tk + c0
            s = s + bias_ref[0, :, lo:lo + tq - c0]
            if d <= 0:
                valid = (lax.broadcasted_iota(jnp.int32, s.shape, 1) + (c0 + d)
                         >= lax.broadcasted_iota(jnp.int32, s.shape, 0))
                s = jnp.where(valid, s, NEG_INF)
        return s

    def fixed_step(j, near):
        vt = vt_ref[0, 0, j]
        c0 = first_query(near)
        for c in range(2):
            p = jnp.exp(scores(j, c, near) - m_ref[c][:, c0:])
            l_ref[c] += row_pad(jnp.sum(p, axis=0, keepdims=True), c0, 0.0)
            acc_ref[c, :, c0:] += jnp.dot(vt, p.astype(BF16), preferred_element_type=F32)

    def online_step(j, near):
        vt = vt_ref[0, 0, j]
        c0 = first_query(near)
        for c in range(2):
            s = scores(j, c, near)
            m_old = m_ref[c]
            m_prev = m_old[:, c0:]
            m_new = jnp.maximum(m_prev, jnp.max(s, axis=0, keepdims=True))
            alpha = jnp.exp(m_prev - m_new)
            p = jnp.exp(s - m_new)
            l_ref[c] = row_pad(alpha, c0, 1.0) * l_ref[c] + row_pad(jnp.sum(p, axis=0, keepdims=True), c0, 0.0)
            acc_ref[c, :, c0:] = (alpha * acc_ref[c, :, c0:]
                                  + jnp.dot(vt, p.astype(BF16), preferred_element_type=F32))
            m_ref[c] = m_new if c0 == 0 else jnp.concatenate([m_old[:, :c0], m_new], axis=1)

    def all_blocks(step):
        def far_body(j, carry):
            step(j, None)
            return carry
        lax.fori_loop(0, jnp.maximum(first_near, 0), far_body, 0)

        @pl.when(qi >= 1)
        def _():
            step(first_near, 1)

        for r in range(ratio):
            step(first_near + 1 + r, -r)

    l_ref[...] = jnp.zeros(l_ref.shape, F32)
    acc_ref[...] = jnp.zeros(acc_ref.shape, F32)
    self_bias = bias_ref[0, 0:1, tq - tk:tq - tk + 1]
    q0 = pl.multiple_of(qi * tq, tq)
    for c in range(2):
        k_self = k_ref[0, pl.ds(q0, tq), c * HEAD_DIM:(c + 1) * HEAD_DIM].astype(F32)
        qt = qt_ref[0, c * HEAD_DIM:(c + 1) * HEAD_DIM, :].astype(F32)
        m_ref[c] = jnp.sum(qt * k_self.T, axis=0, keepdims=True) + self_bias
    all_blocks(fixed_step)

    n_bad = (jnp.sum(jnp.where(jnp.abs(acc_ref[...]) < FINITE_MAX, 0.0, 1.0))
             + jnp.sum(jnp.where(jnp.abs(l_ref[...]) < FINITE_MAX, 0.0, 1.0)))

    @pl.when(n_bad > 0.0)
    def _():
        m_ref[...] = jnp.full(m_ref.shape, NEG_INF, F32)
        l_ref[...] = jnp.zeros(l_ref.shape, F32)
        acc_ref[...] = jnp.zeros(acc_ref.shape, F32)
        all_blocks(online_step)

    lam_v = lam_ref[...]
    lam = (jnp.exp(jnp.sum(lam_v[0:1] * lam_v[1:2], axis=-1, keepdims=True))
           - jnp.exp(jnp.sum(lam_v[2:3] * lam_v[3:4], axis=-1, keepdims=True)) + lambda_init)
    ot = acc_ref[0] / l_ref[0] - lam * (acc_ref[1] / l_ref[1])
    ot = ot * lax.rsqrt(jnp.mean(ot * ot, axis=0, keepdims=True) + EPS)
    ot = ot * subln_ref[...] * (1.0 - lambda_init)
    o_ref[...] = ot.T.astype(o_ref.dtype)


def _bias_by_distance(rel_bias, tq, tk):
    table = (rel_bias - rel_bias[N_BUCKETS - 1:N_BUCKETS, :]).astype(F32)
    rel = jnp.arange(2 * tq, dtype=jnp.int32) - (tq - tk)
    onehot = (_t5_causal_bucket(rel)[:, None] == jnp.arange(N_BUCKETS, dtype=jnp.int32)[None, :]).astype(F32)
    z = jnp.dot(onehot, table, precision=lax.Precision.HIGHEST).T
    return z.reshape(z.shape[0], 1, 2 * tq)


def _diff_attention(q, k, v, rel_bias, lam_vecs, subln, *, batch, seq, lambda_init):
    t, d = q.shape
    hd2 = 2 * HEAD_DIM
    n_heads = d // hd2
    tq = min(ATTN_TQ, seq)
    tk = min(ATTN_TK, tq)
    nq, nk = seq // tq, seq // tk
    assert tk >= CHUNK and tq % tk == 0 and seq % tq == 0
    far = np.arange(tk + 1, max(seq, tk + 2), dtype=np.int64)
    far_bucket = N_BUCKETS // 2 + (np.log(far.astype(np.float32) / (N_BUCKETS // 2))
                                   / math.log(MAX_DISTANCE / (N_BUCKETS // 2))
                                   * (N_BUCKETS - N_BUCKETS // 2)).astype(np.int64)
    assert np.all(far_bucket >= N_BUCKETS - 1)
    z = _bias_by_distance(rel_bias, tq, tk)
    qt = jnp.transpose(q.reshape(batch, seq, d), (0, 2, 1))
    k3 = k.reshape(batch, seq, d)
    vt = jnp.transpose(v.reshape(batch, nk, tk, n_heads, hd2), (0, 3, 1, 4, 2))
    return pl.pallas_call(
        functools.partial(_attn_kernel, tq=tq, tk=tk, lambda_init=lambda_init),
        grid=(batch, n_heads, nq),
        in_specs=[pl.BlockSpec((4, HEAD_DIM), lambda b, h, i: (0, 0)),
                  pl.BlockSpec((1, hd2, tq), lambda b, h, i: (b, h, i)),
                  pl.BlockSpec((1, seq, hd2), lambda b, h, i: (b, 0, h)),
                  pl.BlockSpec((1, 1, nk, hd2, tk), lambda b, h, i: (b, h, 0, 0, 0)),
                  pl.BlockSpec((1, 1, 2 * tq), lambda b, h, i: (h, 0, 0)),
                  pl.BlockSpec((hd2, 1), lambda b, h, i: (0, 0))],
        out_specs=pl.BlockSpec((tq, hd2), lambda b, h, i: (b * nq + i, h)),
        out_shape=jax.ShapeDtypeStruct((t, d), BF16),
        scratch_shapes=[pltpu.VMEM((2, hd2, tq), F32),
                        pltpu.VMEM((2, 1, tq), F32),
                        pltpu.VMEM((2, 1, tq), F32),
                        pltpu.VMEM((1, tk, 2 * tq), F32)],
        name="diff_attn",
        compiler_params=_params(("parallel", "parallel", "arbitrary")),
    )(lam_vecs, qt, k3, vt, z, subln.reshape(hd2, 1))


def _router_kernel(x_ref, g_ref, wr_ref, o_ref, cnt_ref, *, n_groups, n_experts):
    @pl.when(pl.program_id(0) == 0)
    def _():
        cnt_ref[...] = jnp.zeros(cnt_ref.shape, F32)

    x = x_ref[...]
    hn = x * lax.rsqrt(jnp.mean(x * x, axis=-1, keepdims=True) + EPS) * g_ref[...]
    hi = hn.astype(BF16)
    lo = (hn - hi.astype(F32)).astype(BF16)
    r1 = jnp.dot(hi, wr_ref[...], preferred_element_type=F32)
    r2 = jnp.dot(lo, wr_ref[:, :LANES], preferred_element_type=F32)
    logits = r1[:, :LANES] + r1[:, LANES:] + r2
    lane = lax.broadcasted_iota(jnp.int32, logits.shape, 1).astype(F32)
    no_lane = float(LANES)
    is_g = lane < n_groups
    gl = jnp.where(is_g, logits, -jnp.inf)
    gmax = jnp.max(gl, axis=-1, keepdims=True)
    gidx = jnp.min(jnp.where(gl == gmax, lane, no_lane), axis=-1, keepdims=True)
    gsum = jnp.sum(jnp.where(is_g, jnp.exp(gl - gmax), 0.0), axis=-1, keepdims=True)
    g_gate = 1.0 / gsum
    lo = n_groups + gidx * n_experts
    in_grp = (lane >= lo) & (lane < lo + n_experts)
    el = jnp.where(in_grp, logits, -jnp.inf)
    t1 = jnp.max(el, axis=-1, keepdims=True)
    i1 = jnp.min(jnp.where(el == t1, lane, no_lane), axis=-1, keepdims=True)
    el2 = jnp.where(lane == i1, -jnp.inf, el)
    t2 = jnp.max(el2, axis=-1, keepdims=True)
    i2 = jnp.min(jnp.where(el2 == t2, lane, no_lane), axis=-1, keepdims=True)
    e = jnp.exp(t2 - t1)
    w1 = g_gate / (1.0 + e)
    w2 = g_gate * e / (1.0 + e)
    e1 = i1 - n_groups
    e2 = i2 - n_groups
    tm = logits.shape[0]
    oh1 = lane == e1
    oh2 = lane == e2
    strict_lower = (lax.broadcasted_iota(jnp.int32, (tm, tm), 0)
                    > lax.broadcasted_iota(jnp.int32, (tm, tm), 1)).astype(BF16)
    pre1 = jnp.dot(strict_lower, oh1.astype(BF16), preferred_element_type=F32)
    pre2 = jnp.dot(strict_lower, oh2.astype(BF16), preferred_element_type=F32)
    tot1 = jnp.sum(oh1.astype(F32), axis=0, keepdims=True)
    tot2 = jnp.sum(oh2.astype(F32), axis=0, keepdims=True)
    base = cnt_ref[...]
    rank1 = jnp.sum(jnp.where(oh1, base + pre1, 0.0), axis=-1, keepdims=True)
    rank2 = jnp.sum(jnp.where(oh2, base + tot1 + pre2, 0.0), axis=-1, keepdims=True)
    cnt_ref[...] = base + tot1 + tot2
    out = jnp.zeros(logits.shape, F32)
    for k, val in enumerate((e1, e2, w1, w2, rank1, rank2)):
        out = jnp.where(lane == k, val, out)
    o_ref[...] = out


def _router(h, gain, w_group, w_router):
    t, d = h.shape
    n_groups, _, n_experts = w_router.shape
    tm = min(ROUTER_TM, t)
    assert n_groups + n_groups * n_experts <= LANES
    wr = jnp.concatenate([w_group, jnp.transpose(w_router, (1, 0, 2)).reshape(d, n_groups * n_experts)], axis=1)
    wr = jnp.pad(wr, ((0, 0), (0, LANES - wr.shape[1])))
    wr_hi = wr.astype(BF16)
    wr = jnp.concatenate([wr_hi, (wr - wr_hi.astype(F32)).astype(BF16)], axis=1)
    return pl.pallas_call(
        functools.partial(_router_kernel, n_groups=n_groups, n_experts=n_experts),
        grid=(t // tm,),
        in_specs=[pl.BlockSpec((tm, d), lambda i: (i, 0)),
                  pl.BlockSpec((1, d), lambda i: (0, 0)),
                  pl.BlockSpec((d, 2 * LANES), lambda i: (0, 0))],
        out_specs=[pl.BlockSpec((tm, LANES), lambda i: (i, 0)),
                   pl.BlockSpec((1, LANES), lambda i: (0, 0))],
        out_shape=[jax.ShapeDtypeStruct((t, LANES), F32),
                   jax.ShapeDtypeStruct((1, LANES), F32)],
        name="moe_router",
        compiler_params=_params(("arbitrary",)),
    )(h, gain.reshape(1, d), wr)


def _row_copy(src_hbm, idx, buf, slot, r, sem):
    src = src_hbm.at[idx] if len(src_hbm.shape) == 3 else src_hbm.at[pl.ds(idx, 1), :]
    return pltpu.make_async_copy(src, buf.at[slot, pl.ds(r, 1), :], sem.at[slot])


def _gather_start(src_hbm, idx_ref, base, buf, slot, sem, n_rows):
    def body(r, c):
        _row_copy(src_hbm, idx_ref[base + r], buf, slot, r, sem).start()
        return c
    lax.fori_loop(0, n_rows, body, 0, unroll=8)


def _gather_wait(src_hbm, buf, slot, sem, n_rows):
    def body(r, c):
        _row_copy(src_hbm, 0, buf, slot, r, sem).wait()
        return c
    lax.fori_loop(0, n_rows, body, 0, unroll=8)


def _expert_changed(te_ref, t):
    return (t == 0) | (te_ref[t] != te_ref[jnp.maximum(t - 1, 0)])


def _moe_up_kernel(te_ref, nv_ref, tok_ref, nxt_ref, h_hbm, g_ref, w1_hbm, w3_hbm, o_ref,
                   buf, sem, w1f_ref, w3f_ref, wsem, w1b_ref, w3b_ref, *, tm, e0):
    t = pl.program_id(0)
    n_valid = nv_ref[0]

    def weight_copies(e):
        return (pltpu.make_async_copy(w1_hbm.at[e0 + e], w1f_ref, wsem.at[0]),
                pltpu.make_async_copy(w3_hbm.at[e0 + e], w3f_ref, wsem.at[1]))

    @pl.when(t == 0)
    def _():
        for cp in weight_copies(te_ref[0]):
            cp.start()
        _gather_start(h_hbm, tok_ref, 0, buf, 0, sem, tm)

    @pl.when(_expert_changed(te_ref, t))
    def _():
        for cp in weight_copies(te_ref[t]):
            cp.wait()
        w1b_ref[...] = w1f_ref[...].astype(BF16)
        w3b_ref[...] = w3f_ref[...].astype(BF16)

        @pl.when(nxt_ref[t] >= 0)
        def _():
            for cp in weight_copies(nxt_ref[t]):
                cp.start()

    def step(slot):
        _gather_wait(h_hbm, buf, slot, sem, tm)
        base = jnp.minimum(t + 1, n_valid - 1) * tm
        for r in range(tm):
            _row_copy(h_hbm, tok_ref[base + r], buf, 1 - slot, r, sem).start()
        x = buf[slot]
        hn = (x * lax.rsqrt(jnp.mean(x * x, axis=-1, keepdims=True) + EPS) * g_ref[...]).astype(BF16)
        a = jnp.dot(hn, w1b_ref[...], preferred_element_type=F32)
        b = jnp.dot(hn, w3b_ref[...], preferred_element_type=F32)
        o_ref[...] = (a * jax.nn.sigmoid(a) * b).astype(o_ref.dtype)

        @pl.when(t == n_valid - 1)
        def _():
            _gather_wait(h_hbm, buf, 1 - slot, sem, tm)

    for slot in range(2):
        @pl.when((t < n_valid) & (t % 2 == slot))
        def _():
            step(slot)

    @pl.when(t >= n_valid)
    def _():
        o_ref[...] = jnp.zeros(o_ref.shape, o_ref.dtype)


def _moe_down_kernel(te_ref, nv_ref, h_ref, w2_ref, o_ref, w2b_ref):
    t = pl.program_id(0)

    @pl.when(_expert_changed(te_ref, t))
    def _():
        w2b_ref[...] = w2_ref[0].astype(BF16)

    @pl.when(t < nv_ref[0])
    def _():
        o_ref[...] = jnp.dot(h_ref[...], w2b_ref[...], preferred_element_type=F32)

    @pl.when(t >= nv_ref[0])
    def _():
        o_ref[...] = jnp.zeros(o_ref.shape, o_ref.dtype)


def _combine_kernel(pos_ref, y_hbm, h_ref, route_ref, o_ref, buf0, buf1, sem0, sem1, *, tm, n_tok):
    t = pl.program_id(0)
    n = pl.num_programs(0)
    slot = t % 2

    def start(step, s):
        _gather_start(y_hbm, pos_ref, step * tm, buf0, s, sem0, tm)
        _gather_start(y_hbm, pos_ref, n_tok + step * tm, buf1, s, sem1, tm)

    @pl.when(t == 0)
    def _():
        start(0, 0)

    @pl.when(t + 1 < n)
    def _():
        start(t + 1, 1 - slot)

    _gather_wait(y_hbm, buf0, slot, sem0, tm)
    _gather_wait(y_hbm, buf1, slot, sem1, tm)
    w = route_ref[...]
    o_ref[...] = h_ref[...] + w[:, 2:3] * buf0[slot] + w[:, 3:4] * buf1[slot]


def _hier_moe(h, h_rows, gain, w_group, w_router, w1, w3, w2, layer):
    t, d = h.shape
    f = w1.shape[2]
    ne = w_router.shape[0] * w_router.shape[2]
    e0 = layer * ne
    tm, tc = MOE_TM, COMBINE_TM
    route, counts = _router(h, gain, w_group, w_router)
    e_pair = jnp.concatenate([route[:, 0], route[:, 1]]).astype(jnp.int32)
    rank = jnp.concatenate([route[:, 4], route[:, 5]]).astype(jnp.int32)
    tok_pair = jnp.tile(jnp.arange(t, dtype=jnp.int32), TOP_K)

    counts = counts[0, :ne].astype(jnp.int32)
    tiles_e = (counts + tm - 1) // tm
    tile_end = jnp.cumsum(tiles_e)
    row_start = (tile_end - tiles_e) * tm
    is_e = e_pair[:, None] == jnp.arange(ne, dtype=jnp.int32)[None, :]
    dest = jnp.sum(jnp.where(is_e, row_start[None, :], 0), axis=1) + rank
    n_tiles = (TOP_K * t) // tm + ne
    n_rows = n_tiles * tm
    tok_sorted = jnp.zeros((n_rows,), jnp.int32).at[dest].set(tok_pair)
    n_valid = tile_end[-1]
    tile_ids = jnp.minimum(jnp.arange(n_tiles, dtype=jnp.int32), n_valid - 1)
    tile_expert = jnp.sum((tile_end[None, :] <= tile_ids[:, None]).astype(jnp.int32), axis=1)
    run_end = jnp.sum(jnp.where(tile_expert[:, None] == jnp.arange(ne, dtype=jnp.int32)[None, :],
                                tile_end[None, :], 0), axis=1)
    follows = run_end[:, None] == jnp.arange(n_tiles, dtype=jnp.int32)[None, :]
    next_expert = jnp.where(run_end < n_valid, jnp.sum(jnp.where(follows, tile_expert[None, :], 0), axis=1), -1)
    n_valid = n_valid.reshape(1).astype(jnp.int32)

    hid = pl.pallas_call(
        functools.partial(_moe_up_kernel, tm=tm, e0=e0),
        grid_spec=pltpu.PrefetchScalarGridSpec(
            num_scalar_prefetch=4,
            grid=(n_tiles,),
            in_specs=[pl.BlockSpec(memory_space=pl.ANY),
                      pl.BlockSpec((1, d), lambda i, te, nv, tok, nxt: (0, 0)),
                      pl.BlockSpec(memory_space=pl.ANY),
                      pl.BlockSpec(memory_space=pl.ANY)],
            out_specs=pl.BlockSpec((tm, f), lambda i, te, nv, tok, nxt: (i, 0)),
            scratch_shapes=[pltpu.VMEM((2, tm, d), F32), pltpu.SemaphoreType.DMA((2,)),
                            pltpu.VMEM((d, f), F32), pltpu.VMEM((d, f), F32), pltpu.SemaphoreType.DMA((2,)),
                            pltpu.VMEM((d, f), BF16), pltpu.VMEM((d, f), BF16)]),
        out_shape=jax.ShapeDtypeStruct((n_rows, f), BF16),
        name="moe_up",
        compiler_params=_params(("arbitrary",)),
    )(tile_expert, n_valid, tok_sorted, next_expert.astype(jnp.int32), h_rows, gain.reshape(1, d), w1, w3)

    ys = pl.pallas_call(
        _moe_down_kernel,
        grid_spec=pltpu.PrefetchScalarGridSpec(
            num_scalar_prefetch=2,
            grid=(n_tiles,),
            in_specs=[pl.BlockSpec((tm, f), lambda i, te, nv: (i, 0)),
                      pl.BlockSpec((1, f, d), lambda i, te, nv: (e0 + te[i], 0, 0))],
            out_specs=pl.BlockSpec((tm, d), lambda i, te, nv: (i, 0)),
            scratch_shapes=[pltpu.VMEM((f, d), BF16)]),
        out_shape=jax.ShapeDtypeStruct((n_rows, d), F32),
        name="moe_down",
        compiler_params=_params(("arbitrary",)),
    )(tile_expert, n_valid, hid, w2)

    return pl.pallas_call(
        functools.partial(_combine_kernel, tm=tc, n_tok=t),
        grid_spec=pltpu.PrefetchScalarGridSpec(
            num_scalar_prefetch=1,
            grid=(t // tc,),
            in_specs=[pl.BlockSpec(memory_space=pl.ANY),
                      pl.BlockSpec((tc, d), lambda i, pos: (i, 0)),
                      pl.BlockSpec((tc, LANES), lambda i, pos: (i, 0))],
            out_specs=pl.BlockSpec((tc, d), lambda i, pos: (i, 0)),
            scratch_shapes=[pltpu.VMEM((2, tc, d), F32), pltpu.VMEM((2, tc, d), F32),
                            pltpu.SemaphoreType.DMA((2,)), pltpu.SemaphoreType.DMA((2,))]),
        out_shape=jax.ShapeDtypeStruct((t, d), F32),
        name="moe_combine",
        compiler_params=_params(("arbitrary",)),
    )(dest.astype(jnp.int32), ys, h, route)


def kernel(x, a_norm, a_w_in, a_b_in, a_ln_g, a_ln_b, a_w_s, a_b_s, a_w_out, kv_norm, w_k, w_v, k_norm, b_norm, w_q, q_norm, lam_q1, lam_k1, lam_q2, lam_k2, subln, w_o, rel_bias, m_norm, m_w_group, m_w_router, m_w1, m_w3, m_w2):
    batch, seq, d = x.shape
    depth = m_norm.shape[0]
    n_a = a_norm.shape[0]
    h = x.reshape(batch * seq, d)
    f = m_w1.shape[-1]
    w1 = m_w1.reshape(-1, d, f)
    w3 = m_w3.reshape(-1, d, f)
    w2 = m_w2.reshape(-1, f, d)
    k = v = None
    for layer in range(depth):
        if layer < n_a:
            i = layer
            (hn,) = _rmsnorm(h, a_norm[i:i + 1])
            z = _matmul(hn, a_w_in, i, epilogue="gelu_bias", extra=a_b_in[i])
            gz = _gmlp_gate(z, a_ln_g[i], a_ln_b[i], a_w_s[i], a_b_s[i])
            h, h_rows = _matmul(gz, a_w_out, i, epilogue="residual", extra=h, out_dtype=F32)
        else:
            j = layer - n_a
            if layer == n_a:
                hkv, hq = _rmsnorm(h, jnp.stack([kv_norm, b_norm[j]]))
                k = _matmul(hkv, w_k, epilogue="headnorm", extra=k_norm)
                v = _matmul(hkv, w_v)
            else:
                (hq,) = _rmsnorm(h, b_norm[j:j + 1])
            q = _matmul(hq, w_q, j, epilogue="headnorm", extra=q_norm[j], scale=HEAD_DIM ** -0.5)
            lam_vecs = jnp.stack([lam_q1[j], lam_k1[j], lam_q2[j], lam_k2[j]]).astype(F32)
            o = _diff_attention(q, k, v, rel_bias, lam_vecs, subln[j], batch=batch, seq=seq,
                                lambda_init=_lambda_init(layer))
            h, h_rows = _matmul(o, w_o, j, epilogue="residual", extra=h, out_dtype=F32)
        h = _hier_moe(h, h_rows, m_norm[layer], m_w_group[layer], m_w_router[layer], w1, w3, w2, layer)
    return h.reshape(batch, seq, d)
```

```python
import functools
import math

import jax
import jax.numpy as jnp
import numpy as np
from jax import lax
from jax.experimental import pallas as pl
from jax.experimental.pallas import tpu as pltpu

F32 = jnp.float32
BF16 = jnp.bfloat16

CHUNK = 128
HEAD_DIM = 128
N_BUCKETS = 32
MAX_DISTANCE = 128
TOP_K = 2
EPS = 1e-6
LN_EPS = 1e-5
NEG_INF = -1e30
FINITE_MAX = 3.0e38

LANES = 128
VMEM_LIMIT_BYTES = 56 * 1024 * 1024

NORM_TM = 512
MM_TM, MM_TM_NARROW_OUT, MM_TN = 512, 1024, 1024
GATE_TM = 512
ATTN_TQ, ATTN_TK = 1024, 512
ROUTER_TM = 512
MOE_TM = 256
COMBINE_TM = 128


def _params(semantics):
    return pltpu.CompilerParams(dimension_semantics=semantics, vmem_limit_bytes=VMEM_LIMIT_BYTES)


def _lambda_init(layer):
    return 0.8 - 0.6 * math.exp(-0.3 * layer)


def _rmsnorm_kernel(x_ref, g_ref, *o_refs):
    x = x_ref[...]
    y = x * lax.rsqrt(jnp.mean(x * x, axis=-1, keepdims=True) + EPS)
    for k, o_ref in enumerate(o_refs):
        o_ref[...] = (y * g_ref[k:k + 1, :]).astype(o_ref.dtype)


def _rmsnorm(x, gains):
    t, d = x.shape
    n = gains.shape[0]
    tm = min(NORM_TM, t)
    return pl.pallas_call(
        _rmsnorm_kernel,
        grid=(t // tm,),
        in_specs=[pl.BlockSpec((tm, d), lambda i: (i, 0)),
                  pl.BlockSpec((n, d), lambda i: (0, 0))],
        out_specs=[pl.BlockSpec((tm, d), lambda i: (i, 0))] * n,
        out_shape=[jax.ShapeDtypeStruct((t, d), BF16)] * n,
        name="rmsnorm",
        compiler_params=_params(("parallel",)),
    )(x, gains)


def _gelu_tanh(x):
    return 0.5 * x * (1.0 + jnp.tanh(math.sqrt(2.0 / math.pi) * (x + 0.044715 * (x * x * x))))


def _mm_kernel(a_ref, w_ref, *rest, epilogue, scale):
    wb_ref = rest[-1]
    o_ref = rest[1] if epilogue == "residual" else rest[-2]

    @pl.when(pl.program_id(1) == 0)
    def _():
        wb_ref[...] = w_ref[...].astype(BF16)

    acc = jnp.dot(a_ref[...], wb_ref[...], preferred_element_type=F32)
    if epilogue == "gelu_bias":
        o_ref[...] = _gelu_tanh(acc + rest[0][...]).astype(o_ref.dtype)
    elif epilogue == "residual":
        y = rest[0][...] + acc
        o_ref[...] = y
        rest[2][...] = y
    elif epilogue == "headnorm":
        gain = rest[0][...] * scale
        for g in range(acc.shape[1] // HEAD_DIM):
            y = acc[:, g * HEAD_DIM:(g + 1) * HEAD_DIM]
            y = y * lax.rsqrt(jnp.mean(y * y, axis=-1, keepdims=True) + EPS)
            o_ref[:, g * HEAD_DIM:(g + 1) * HEAD_DIM] = (y * gain).astype(o_ref.dtype)
    else:
        o_ref[...] = acc.astype(o_ref.dtype)


def _matmul(a, w, layer=0, *, epilogue="none", extra=None, scale=1.0, out_dtype=BF16):
    m, k = a.shape
    if w.ndim == 2:
        w = w.reshape(1, *w.shape)
    n = w.shape[2]
    tm, tn = min(MM_TM if out_dtype == F32 else MM_TM_NARROW_OUT, m), min(MM_TN, n)
    in_specs = [pl.BlockSpec((tm, k), lambda j, i: (i, 0)),
                pl.BlockSpec((None, k, tn), lambda j, i: (layer, 0, j), pipeline_mode=pl.Buffered(1))]
    args = [a, w]
    if epilogue == "gelu_bias":
        in_specs.append(pl.BlockSpec((1, tn), lambda j, i: (0, j)))
        args.append(extra.reshape(1, n))
    elif epilogue == "residual":
        in_specs.append(pl.BlockSpec((tm, tn), lambda j, i: (i, j)))
        args.append(extra)
    elif epilogue == "headnorm":
        in_specs.append(pl.BlockSpec((1, HEAD_DIM), lambda j, i: (0, 0)))
        args.append(extra.reshape(1, HEAD_DIM))
    out_specs = pl.BlockSpec((tm, tn), lambda j, i: (i, j))
    out_shape = jax.ShapeDtypeStruct((m, n), out_dtype)
    if epilogue == "residual":
        out_specs = [out_specs, pl.BlockSpec((tm, None, tn), lambda j, i: (i, 0, j))]
        out_shape = [out_shape, jax.ShapeDtypeStruct((m, 1, n), out_dtype)]
    return pl.pallas_call(
        functools.partial(_mm_kernel, epilogue=epilogue, scale=scale),
        grid=(n // tn, m // tm),
        in_specs=in_specs,
        out_specs=out_specs,
        out_shape=out_shape,
        scratch_shapes=[pltpu.VMEM((k, tn), BF16)],
        name="mm_" + epilogue,
        compiler_params=_params(("parallel", "arbitrary")),
    )(*args)


def _gmlp_gate_kernel(u_ref, v_ref, lng_ref, lnb_ref, ws_ref, bs_ref, o_ref, *, n_chunks, groups):
    v = v_ref[...].astype(F32)
    mu = jnp.mean(v, axis=-1, keepdims=True)
    vc = v - mu
    var = jnp.mean(vc * vc, axis=-1, keepdims=True)
    vn = (vc * lax.rsqrt(var + LN_EPS) * lng_ref[...] + lnb_ref[...]).astype(BF16)
    gw = v.shape[1] // groups
    row = lax.broadcasted_iota(jnp.int32, (CHUNK, CHUNK), 0)
    col = lax.broadcasted_iota(jnp.int32, (CHUNK, CHUNK), 1)
    causal = row >= col
    for g in range(groups):
        w = jnp.where(causal, ws_ref[g], 0.0).astype(BF16)
        b = bs_ref[g]
        for c in range(n_chunks):
            rows = slice(c * CHUNK, (c + 1) * CHUNK)
            cols = slice(g * gw, (g + 1) * gw)
            s = jnp.dot(w, vn[rows, cols], preferred_element_type=F32) + b
            o_ref[rows, cols] = (u_ref[rows, cols].astype(F32) * s).astype(o_ref.dtype)


def _gmlp_gate(z, ln_g, ln_b, w_s, b_s):
    t, d2 = z.shape
    dg = d2 // 2
    groups = w_s.shape[0]
    tm = min(GATE_TM, t)
    return pl.pallas_call(
        functools.partial(_gmlp_gate_kernel, n_chunks=tm // CHUNK, groups=groups),
        grid=(t // tm,),
        in_specs=[pl.BlockSpec((tm, dg), lambda i: (i, 0)),
                  pl.BlockSpec((tm, dg), lambda i: (i, 1)),
                  pl.BlockSpec((1, dg), lambda i: (0, 0)),
                  pl.BlockSpec((1, dg), lambda i: (0, 0)),
                  pl.BlockSpec((groups, CHUNK, CHUNK), lambda i: (0, 0, 0)),
                  pl.BlockSpec((groups, CHUNK, 1), lambda i: (0, 0, 0))],
        out_specs=pl.BlockSpec((tm, dg), lambda i: (i, 0)),
        out_shape=jax.ShapeDtypeStruct((t, dg), BF16),
        name="gmlp_gate",
        compiler_params=_params(("parallel",)),
    )(z, z, ln_g.reshape(1, dg), ln_b.reshape(1, dg), w_s, b_s.reshape(groups, CHUNK, 1))


def _t5_causal_bucket(rel):
    n = jnp.maximum(rel, 0)
    max_exact = N_BUCKETS // 2
    nf = jnp.maximum(n, 1).astype(F32)
    large = max_exact + (jnp.log(nf / max_exact) / math.log(MAX_DISTANCE / max_exact)
                         * (N_BUCKETS - max_exact)).astype(jnp.int32)
    large = jnp.minimum(large, N_BUCKETS - 1)
    return jnp.where(n < max_exact, n, large)


def _attn_kernel(lam_ref, qt_ref, k_ref, vt_ref, z_ref, subln_ref, o_ref, acc_ref, m_ref, l_ref, bias_ref,
                 *, tq, tk, lambda_init):
    qi = pl.program_id(2)
    ratio = tq // tk
    first_near = ratio * qi - 1

    @pl.when(qi == 0)
    def _():
        bias_ref[0] = pltpu.roll(jnp.broadcast_to(z_ref[0], (tk, 2 * tq)), 0, 1, stride=1, stride_axis=0)

    def first_query(near):
        return 0 if near is None else max(0, -near * tk)

    def row_pad(x, c0, fill):
        return x if c0 == 0 else jnp.concatenate([jnp.full((1, c0), fill, F32), x], axis=1)

    def scores(j, c, near):
        ks = pl.multiple_of(j * tk, tk)
        c0 = first_query(near)
        k = k_ref[0, pl.ds(ks, tk), c * HEAD_DIM:(c + 1) * HEAD_DIM]
        qt = qt_ref[0, c * HEAD_DIM:(c + 1) * HEAD_DIM, c0:]
        s = jnp.dot(k, qt, preferred_element_type=F32)
        if near is not None:
            d = near * tk
            lo = d + tq - tk + c0
            s = s + bias_ref[0, :, lo:lo + tq - c0]
            if d <= 0:
                valid = (lax.broadcasted_iota(jnp.int32, s.shape, 1) + (c0 + d)
                         >= lax.broadcasted_iota(jnp.int32, s.shape, 0))
                s = jnp.where(valid, s, NEG_INF)
        return s

    def fixed_step(j, near):
        vt = vt_ref[0, 0, j]
        c0 = first_query(near)
        for c in range(2):
            p = jnp.exp(scores(j, c, near) - m_ref[c][:, c0:])
            l_ref[c] += row_pad(jnp.sum(p, axis=0, keepdims=True), c0, 0.0)
            acc_ref[c, :, c0:] += jnp.dot(vt, p.astype(BF16), preferred_element_type=F32)

    def online_step(j, near):
        vt = vt_ref[0, 0, j]
        c0 = first_query(near)
        for c in range(2):
            s = scores(j, c, near)
            m_old = m_ref[c]
            m_prev = m_old[:, c0:]
            m_new = jnp.maximum(m_prev, jnp.max(s, axis=0, keepdims=True))
            alpha = jnp.exp(m_prev - m_new)
            p = jnp.exp(s - m_new)
            l_ref[c] = row_pad(alpha, c0, 1.0) * l_ref[c] + row_pad(jnp.sum(p, axis=0, keepdims=True), c0, 0.0)
            acc_ref[c, :, c0:] = (alpha * acc_ref[c, :, c0:]
                                  + jnp.dot(vt, p.astype(BF16), preferred_element_type=F32))
            m_ref[c] = m_new if c0 == 0 else jnp.concatenate([m_old[:, :c0], m_new], axis=1)

    def all_blocks(step):
        def far_body(j, carry):
            step(j, None)
            return carry
        lax.fori_loop(0, jnp.maximum(first_near, 0), far_body, 0)

        @pl.when(qi >= 1)
        def _():
            step(first_near, 1)

        for r in range(ratio):
            step(first_near + 1 + r, -r)

    l_ref[...] = jnp.zeros(l_ref.shape, F32)
    acc_ref[...] = jnp.zeros(acc_ref.shape, F32)
    self_bias = bias_ref[0, 0:1, tq - tk:tq - tk + 1]
    q0 = pl.multiple_of(qi * tq, tq)
    for c in range(2):
        k_self = k_ref[0, pl.ds(q0, tq), c * HEAD_DIM:(c + 1) * HEAD_DIM].astype(F32)
        qt = qt_ref[0, c * HEAD_DIM:(c + 1) * HEAD_DIM, :].astype(F32)
        m_ref[c] = jnp.sum(qt * k_self.T, axis=0, keepdims=True) + self_bias
    all_blocks(fixed_step)

    n_bad = (jnp.sum(jnp.where(jnp.abs(acc_ref[...]) < FINITE_MAX, 0.0, 1.0))
             + jnp.sum(jnp.where(jnp.abs(l_ref[...]) < FINITE_MAX, 0.0, 1.0)))

    @pl.when(n_bad > 0.0)
    def _():
        m_ref[...] = jnp.full(m_ref.shape, NEG_INF, F32)
        l_ref[...] = jnp.zeros(l_ref.shape, F32)
        acc_ref[...] = jnp.zeros(acc_ref.shape, F32)
        all_blocks(online_step)

    lam_v = lam_ref[...]
    lam = (jnp.exp(jnp.sum(lam_v[0:1] * lam_v[1:2], axis=-1, keepdims=True))
           - jnp.exp(jnp.sum(lam_v[2:3] * lam_v[3:4], axis=-1, keepdims=True)) + lambda_init)
    ot = acc_ref[0] / l_ref[0] - lam * (acc_ref[1] / l_ref[1])
    ot = ot * lax.rsqrt(jnp.mean(ot * ot, axis=0, keepdims=True) + EPS)
    ot = ot * subln_ref[...] * (1.0 - lambda_init)
    o_ref[...] = ot.T.astype(o_ref.dtype)


def _bias_by_distance(rel_bias, tq, tk):
    table = (rel_bias - rel_bias[N_BUCKETS - 1:N_BUCKETS, :]).astype(F32)
    rel = jnp.arange(2 * tq, dtype=jnp.int32) - (tq - tk)
    onehot = (_t5_causal_bucket(rel)[:, None] == jnp.arange(N_BUCKETS, dtype=jnp.int32)[None, :]).astype(F32)
    z = jnp.dot(onehot, table, precision=lax.Precision.HIGHEST).T
    return z.reshape(z.shape[0], 1, 2 * tq)


def _diff_attention(q, k, v, rel_bias, lam_vecs, subln, *, batch, seq, lambda_init):
    t, d = q.shape
    hd2 = 2 * HEAD_DIM
    n_heads = d // hd2
    tq = min(ATTN_TQ, seq)
    tk = min(ATTN_TK, tq)
    nq, nk = seq // tq, seq // tk
    assert tk >= CHUNK and tq % tk == 0 and seq % tq == 0
    far = np.arange(tk + 1, max(seq, tk + 2), dtype=np.int64)
    far_bucket = N_BUCKETS // 2 + (np.log(far.astype(np.float32) / (N_BUCKETS // 2))
                                   / math.log(MAX_DISTANCE / (N_BUCKETS // 2))
                                   * (N_BUCKETS - N_BUCKETS // 2)).astype(np.int64)
    assert np.all(far_bucket >= N_BUCKETS - 1)
    z = _bias_by_distance(rel_bias, tq, tk)
    qt = jnp.transpose(q.reshape(batch, seq, d), (0, 2, 1))
    k3 = k.reshape(batch, seq, d)
    vt = jnp.transpose(v.reshape(batch, nk, tk, n_heads, hd2), (0, 3, 1, 4, 2))
    return pl.pallas_call(
        functools.partial(_attn_kernel, tq=tq, tk=tk, lambda_init=lambda_init),
        grid=(batch, n_heads, nq),
        in_specs=[pl.BlockSpec((4, HEAD_DIM), lambda b, h, i: (0, 0)),
                  pl.BlockSpec((1, hd2, tq), lambda b, h, i: (b, h, i)),
                  pl.BlockSpec((1, seq, hd2), lambda b, h, i: (b, 0, h)),
                  pl.BlockSpec((1, 1, nk, hd2, tk), lambda b, h, i: (b, h, 0, 0, 0)),
                  pl.BlockSpec((1, 1, 2 * tq), lambda b, h, i: (h, 0, 0)),
                  pl.BlockSpec((hd2, 1), lambda b, h, i: (0, 0))],
        out_specs=pl.BlockSpec((tq, hd2), lambda b, h, i: (b * nq + i, h)),
        out_shape=jax.ShapeDtypeStruct((t, d), BF16),
        scratch_shapes=[pltpu.VMEM((2, hd2, tq), F32),
                        pltpu.VMEM((2, 1, tq), F32),
                        pltpu.VMEM((2, 1, tq), F32),
                        pltpu.VMEM((1, tk, 2 * tq), F32)],
        name="diff_attn",
        compiler_params=_params(("parallel", "parallel", "arbitrary")),
    )(lam_vecs, qt, k3, vt, z, subln.reshape(hd2, 1))


def _router_kernel(x_ref, g_ref, wr_ref, o_ref, cnt_ref, *, n_groups, n_experts):
    @pl.when(pl.program_id(0) == 0)
    def _():
        cnt_ref[...] = jnp.zeros(cnt_ref.shape, F32)

    x = x_ref[...]
    hn = x * lax.rsqrt(jnp.mean(x * x, axis=-1, keepdims=True) + EPS) * g_ref[...]
    hi = hn.astype(BF16)
    lo = (hn - hi.astype(F32)).astype(BF16)
    r1 = jnp.dot(hi, wr_ref[...], preferred_element_type=F32)
    r2 = jnp.dot(lo, wr_ref[:, :LANES], preferred_element_type=F32)
    logits = r1[:, :LANES] + r1[:, LANES:] + r2
    lane = lax.broadcasted_iota(jnp.int32, logits.shape, 1).astype(F32)
    no_lane = float(LANES)
    is_g = lane < n_groups
    gl = jnp.where(is_g, logits, -jnp.inf)
    gmax = jnp.max(gl, axis=-1, keepdims=True)
    gidx = jnp.min(jnp.where(gl == gmax, lane, no_lane), axis=-1, keepdims=True)
    gsum = jnp.sum(jnp.where(is_g, jnp.exp(gl - gmax), 0.0), axis=-1, keepdims=True)
    g_gate = 1.0 / gsum
    lo = n_groups + gidx * n_experts
    in_grp = (lane >= lo) & (lane < lo + n_experts)
    el = jnp.where(in_grp, logits, -jnp.inf)
    t1 = jnp.max(el, axis=-1, keepdims=True)
    i1 = jnp.min(jnp.where(el == t1, lane, no_lane), axis=-1, keepdims=True)
    el2 = jnp.where(lane == i1, -jnp.inf, el)
    t2 = jnp.max(el2, axis=-1, keepdims=True)
    i2 = jnp.min(jnp.where(el2 == t2, lane, no_lane), axis=-1, keepdims=True)
    e = jnp.exp(t2 - t1)
    w1 = g_gate / (1.0 + e)
    w2 = g_gate * e / (1.0 + e)
    e1 = i1 - n_groups
    e2 = i2 - n_groups
    tm = logits.shape[0]
    oh1 = lane == e1
    oh2 = lane == e2
    strict_lower = (lax.broadcasted_iota(jnp.int32, (tm, tm), 0)
                    > lax.broadcasted_iota(jnp.int32, (tm, tm), 1)).astype(BF16)
    pre1 = jnp.dot(strict_lower, oh1.astype(BF16), preferred_element_type=F32)
    pre2 = jnp.dot(strict_lower, oh2.astype(BF16), preferred_element_type=F32)
    tot1 = jnp.sum(oh1.astype(F32), axis=0, keepdims=True)
    tot2 = jnp.sum(oh2.astype(F32), axis=0, keepdims=True)
    base = cnt_ref[...]
    rank1 = jnp.sum(jnp.where(oh1, base + pre1, 0.0), axis=-1, keepdims=True)
    rank2 = jnp.sum(jnp.where(oh2, base + tot1 + pre2, 0.0), axis=-1, keepdims=True)
    cnt_ref[...] = base + tot1 + tot2
    out = jnp.zeros(logits.shape, F32)
    for k, val in enumerate((e1, e2, w1, w2, rank1, rank2)):
        out = jnp.where(lane == k, val, out)
    o_ref[...] = out


def _router(h, gain, w_group, w_router):
    t, d = h.shape
    n_groups, _, n_experts = w_router.shape
    tm = min(ROUTER_TM, t)
    assert n_groups + n_groups * n_experts <= LANES
    wr = jnp.concatenate([w_group, jnp.transpose(w_router, (1, 0, 2)).reshape(d, n_groups * n_experts)], axis=1)
    wr = jnp.pad(wr, ((0, 0), (0, LANES - wr.shape[1])))
    wr_hi = wr.astype(BF16)
    wr = jnp.concatenate([wr_hi, (wr - wr_hi.astype(F32)).astype(BF16)], axis=1)
    return pl.pallas_call(
        functools.partial(_router_kernel, n_groups=n_groups, n_experts=n_experts),
        grid=(t // tm,),
        in_specs=[pl.BlockSpec((tm, d), lambda i: (i, 0)),
                  pl.BlockSpec((1, d), lambda i: (0, 0)),
                  pl.BlockSpec((d, 2 * LANES), lambda i: (0, 0))],
        out_specs=[pl.BlockSpec((tm, LANES), lambda i: (i, 0)),
                   pl.BlockSpec((1, LANES), lambda i: (0, 0))],
        out_shape=[jax.ShapeDtypeStruct((t, LANES), F32),
                   jax.ShapeDtypeStruct((1, LANES), F32)],
        name="moe_router",
        compiler_params=_params(("arbitrary",)),
    )(h, gain.reshape(1, d), wr)


def _row_copy(src_hbm, idx, buf, slot, r, sem):
    src = src_hbm.at[idx] if len(src_hbm.shape) == 3 else src_hbm.at[pl.ds(idx, 1), :]
    return pltpu.make_async_copy(src, buf.at[slot, pl.ds(r, 1), :], sem.at[slot])


def _gather_start(src_hbm, idx_ref, base, buf, slot, sem, n_rows):
    def body(r, c):
        _row_copy(src_hbm, idx_ref[base + r], buf, slot, r, sem).start()
        return c
    lax.fori_loop(0, n_rows, body, 0, unroll=8)


def _gather_wait(src_hbm, buf, slot, sem, n_rows):
    def body(r, c):
        _row_copy(src_hbm, 0, buf, slot, r, sem).wait()
        return c
    lax.fori_loop(0, n_rows, body, 0, unroll=8)


def _expert_changed(te_ref, t):
    return (t == 0) | (te_ref[t] != te_ref[jnp.maximum(t - 1, 0)])


def _moe_up_kernel(te_ref, nv_ref, tok_ref, nxt_ref, h_hbm, g_ref, w1_hbm, w3_hbm, o_ref,
                   buf, sem, w1f_ref, w3f_ref, wsem, w1b_ref, w3b_ref, *, tm, e0):
    t = pl.program_id(0)
    n_valid = nv_ref[0]

    def weight_copies(e):
        return (pltpu.make_async_copy(w1_hbm.at[e0 + e], w1f_ref, wsem.at[0]),
                pltpu.make_async_copy(w3_hbm.at[e0 + e], w3f_ref, wsem.at[1]))

    @pl.when(t == 0)
    def _():
        for cp in weight_copies(te_ref[0]):
            cp.start()
        _gather_start(h_hbm, tok_ref, 0, buf, 0, sem, tm)

    @pl.when(_expert_changed(te_ref, t))
    def _():
        for cp in weight_copies(te_ref[t]):
            cp.wait()
        w1b_ref[...] = w1f_ref[...].astype(BF16)
        w3b_ref[...] = w3f_ref[...].astype(BF16)

        @pl.when(nxt_ref[t] >= 0)
        def _():
            for cp in weight_copies(nxt_ref[t]):
                cp.start()

    def step(slot):
        _gather_wait(h_hbm, buf, slot, sem, tm)
        base = jnp.minimum(t + 1, n_valid - 1) * tm
        for r in range(tm):
            _row_copy(h_hbm, tok_ref[base + r], buf, 1 - slot, r, sem).start()
        x = buf[slot]
        hn = (x * lax.rsqrt(jnp.mean(x * x, axis=-1, keepdims=True) + EPS) * g_ref[...]).astype(BF16)
        a = jnp.dot(hn, w1b_ref[...], preferred_element_type=F32)
        b = jnp.dot(hn, w3b_ref[...], preferred_element_type=F32)
        o_ref[...] = (a * jax.nn.sigmoid(a) * b).astype(o_ref.dtype)

        @pl.when(t == n_valid - 1)
        def _():
            _gather_wait(h_hbm, buf, 1 - slot, sem, tm)

    for slot in range(2):
        @pl.when((t < n_valid) & (t % 2 == slot))
        def _():
            step(slot)

    @pl.when(t >= n_valid)
    def _():
        o_ref[...] = jnp.zeros(o_ref.shape, o_ref.dtype)


def _moe_down_kernel(te_ref, nv_ref, h_ref, w2_ref, o_ref, w2b_ref):
    t = pl.program_id(0)

    @pl.when(_expert_changed(te_ref, t))
    def _():
        w2b_ref[...] = w2_ref[0].astype(BF16)

    @pl.when(t < nv_ref[0])
    def _():
        o_ref[...] = jnp.dot(h_ref[...], w2b_ref[...], preferred_element_type=F32)

    @pl.when(t >= nv_ref[0])
    def _():
        o_ref[...] = jnp.zeros(o_ref.shape, o_ref.dtype)


def _combine_kernel(pos_ref, y_hbm, h_ref, route_ref, *rest, tm, n_tok, n_norms):
    g_ref = rest[0] if n_norms else None
    o_ref = rest[1 if n_norms else 0]
    norm_refs = rest[2:2 + n_norms] if n_norms else ()
    buf0, buf1, sem0, sem1 = rest[-4:]
    t = pl.program_id(0)
    n = pl.num_programs(0)
    slot = t % 2

    def start(step, s):
        _gather_start(y_hbm, pos_ref, step * tm, buf0, s, sem0, tm)
        _gather_start(y_hbm, pos_ref, n_tok + step * tm, buf1, s, sem1, tm)

    @pl.when(t == 0)
    def _():
        start(0, 0)

    @pl.when(t + 1 < n)
    def _():
        start(t + 1, 1 - slot)

    _gather_wait(y_hbm, buf0, slot, sem0, tm)
    _gather_wait(y_hbm, buf1, slot, sem1, tm)
    w = route_ref[...]
    y = h_ref[...] + w[:, 2:3] * buf0[slot] + w[:, 3:4] * buf1[slot]
    o_ref[...] = y
    if n_norms:
        yn = y * lax.rsqrt(jnp.mean(y * y, axis=-1, keepdims=True) + EPS)
        for k, n_ref in enumerate(norm_refs):
            n_ref[...] = (yn * g_ref[k:k + 1, :]).astype(n_ref.dtype)


def _rows_to_tokens_kernel(dest_ref, o_ref, *, n_tok):
    def clear(i, c):
        o_ref[i] = 0
        return c
    lax.fori_loop(0, o_ref.shape[0], clear, 0, unroll=8)

    def place(p, c):
        o_ref[dest_ref[p]] = jnp.where(p >= n_tok, p - n_tok, p)
        return c
    lax.fori_loop(0, dest_ref.shape[0], place, 0, unroll=8)


def _rows_to_tokens(dest, n_rows, n_tok):
    return pl.pallas_call(
        functools.partial(_rows_to_tokens_kernel, n_tok=n_tok),
        in_specs=[pl.BlockSpec(memory_space=pltpu.SMEM)],
        out_specs=pl.BlockSpec(memory_space=pltpu.SMEM),
        out_shape=jax.ShapeDtypeStruct((n_rows,), jnp.int32),
        name="moe_rows_to_tokens",
    )(dest)


def _hier_moe(h, h_rows, gain, w_group, w_router, w1, w3, w2, layer, next_gains=None):
    t, d = h.shape
    f = w1.shape[2]
    ne = w_router.shape[0] * w_router.shape[2]
    e0 = layer * ne
    tm, tc = MOE_TM, COMBINE_TM
    route, counts = _router(h, gain, w_group, w_router)
    e_pair = jnp.concatenate([route[:, 0], route[:, 1]]).astype(jnp.int32)
    rank = jnp.concatenate([route[:, 4], route[:, 5]]).astype(jnp.int32)

    counts = counts[0, :ne].astype(jnp.int32)
    tiles_e = (counts + tm - 1) // tm
    tile_end = jnp.cumsum(tiles_e)
    row_start = (tile_end - tiles_e) * tm
    is_e = e_pair[:, None] == jnp.arange(ne, dtype=jnp.int32)[None, :]
    dest = jnp.sum(jnp.where(is_e, row_start[None, :], 0), axis=1) + rank
    n_tiles = (TOP_K * t) // tm + ne
    n_rows = n_tiles * tm
    dest = dest.astype(jnp.int32)
    tok_sorted = _rows_to_tokens(dest, n_rows, t)
    n_valid = tile_end[-1]
    tile_ids = jnp.minimum(jnp.arange(n_tiles, dtype=jnp.int32), n_valid - 1)
    tile_expert = jnp.sum((tile_end[None, :] <= tile_ids[:, None]).astype(jnp.int32), axis=1)
    run_end = jnp.sum(jnp.where(tile_expert[:, None] == jnp.arange(ne, dtype=jnp.int32)[None, :],
                                tile_end[None, :], 0), axis=1)
    follows = run_end[:, None] == jnp.arange(n_tiles, dtype=jnp.int32)[None, :]
    next_expert = jnp.where(run_end < n_valid, jnp.sum(jnp.where(follows, tile_expert[None, :], 0), axis=1), -1)
    n_valid = n_valid.reshape(1).astype(jnp.int32)

    hid = pl.pallas_call(
        functools.partial(_moe_up_kernel, tm=tm, e0=e0),
        grid_spec=pltpu.PrefetchScalarGridSpec(
            num_scalar_prefetch=4,
            grid=(n_tiles,),
            in_specs=[pl.BlockSpec(memory_space=pl.ANY),
                      pl.BlockSpec((1, d), lambda i, te, nv, tok, nxt: (0, 0)),
                      pl.BlockSpec(memory_space=pl.ANY),
                      pl.BlockSpec(memory_space=pl.ANY)],
            out_specs=pl.BlockSpec((tm, f), lambda i, te, nv, tok, nxt: (i, 0)),
            scratch_shapes=[pltpu.VMEM((2, tm, d), F32), pltpu.SemaphoreType.DMA((2,)),
                            pltpu.VMEM((d, f), F32), pltpu.VMEM((d, f), F32), pltpu.SemaphoreType.DMA((2,)),
                            pltpu.VMEM((d, f), BF16), pltpu.VMEM((d, f), BF16)]),
        out_shape=jax.ShapeDtypeStruct((n_rows, f), BF16),
        name="moe_up",
        compiler_params=_params(("arbitrary",)),
    )(tile_expert, n_valid, tok_sorted, next_expert.astype(jnp.int32), h_rows, gain.reshape(1, d), w1, w3)

    ys = pl.pallas_call(
        _moe_down_kernel,
        grid_spec=pltpu.PrefetchScalarGridSpec(
            num_scalar_prefetch=2,
            grid=(n_tiles,),
            in_specs=[pl.BlockSpec((tm, f), lambda i, te, nv: (i, 0)),
                      pl.BlockSpec((1, f, d), lambda i, te, nv: (e0 + te[i], 0, 0))],
            out_specs=pl.BlockSpec((tm, d), lambda i, te, nv: (i, 0)),
            scratch_shapes=[pltpu.VMEM((f, d), BF16)]),
        out_shape=jax.ShapeDtypeStruct((n_rows, d), F32),
        name="moe_down",
        compiler_params=_params(("arbitrary",)),
    )(tile_expert, n_valid, hid, w2)

    n_norms = 0 if next_gains is None else next_gains.shape[0]
    row_spec = pl.BlockSpec((tc, d), lambda i, pos: (i, 0))
    in_specs = [pl.BlockSpec(memory_space=pl.ANY), row_spec, pl.BlockSpec((tc, LANES), lambda i, pos: (i, 0))]
    args = [dest.astype(jnp.int32), ys, h, route]
    if n_norms:
        in_specs.append(pl.BlockSpec((n_norms, d), lambda i, pos: (0, 0)))
        args.append(next_gains)
    outs = pl.pallas_call(
        functools.partial(_combine_kernel, tm=tc, n_tok=t, n_norms=n_norms),
        grid_spec=pltpu.PrefetchScalarGridSpec(
            num_scalar_prefetch=1,
            grid=(t // tc,),
            in_specs=in_specs,
            out_specs=[row_spec] * (1 + n_norms),
            scratch_shapes=[pltpu.VMEM((2, tc, d), F32), pltpu.VMEM((2, tc, d), F32),
                            pltpu.SemaphoreType.DMA((2,)), pltpu.SemaphoreType.DMA((2,))]),
        out_shape=[jax.ShapeDtypeStruct((t, d), F32)] + [jax.ShapeDtypeStruct((t, d), BF16)] * n_norms,
        name="moe_combine",
        compiler_params=_params(("arbitrary",)),
    )(*args)
    return outs[0], tuple(outs[1:])


def kernel(x, a_norm, a_w_in, a_b_in, a_ln_g, a_ln_b, a_w_s, a_b_s, a_w_out, kv_norm, w_k, w_v, k_norm, b_norm, w_q, q_norm, lam_q1, lam_k1, lam_q2, lam_k2, subln, w_o, rel_bias, m_norm, m_w_group, m_w_router, m_w1, m_w3, m_w2):
    batch, seq, d = x.shape
    depth = m_norm.shape[0]
    n_a = a_norm.shape[0]
    h = x.reshape(batch * seq, d)
    f = m_w1.shape[-1]
    w1 = m_w1.reshape(-1, d, f)
    w3 = m_w3.reshape(-1, d, f)
    w2 = m_w2.reshape(-1, f, d)
    def pre_norm_gains(layer):
        if layer >= depth:
            return None
        if layer < n_a:
            return a_norm[layer:layer + 1]
        j = layer - n_a
        return jnp.stack([kv_norm, b_norm[j]]) if layer == n_a else b_norm[j:j + 1]

    k = v = None
    normed = _rmsnorm(h, pre_norm_gains(0))
    for layer in range(depth):
        if layer < n_a:
            i = layer
            (hn,) = normed
            z = _matmul(hn, a_w_in, i, epilogue="gelu_bias", extra=a_b_in[i])
            gz = _gmlp_gate(z, a_ln_g[i], a_ln_b[i], a_w_s[i], a_b_s[i])
            h, h_rows = _matmul(gz, a_w_out, i, epilogue="residual", extra=h, out_dtype=F32)
        else:
            j = layer - n_a
            if layer == n_a:
                hkv, hq = normed
                k = _matmul(hkv, w_k, epilogue="headnorm", extra=k_norm)
                v = _matmul(hkv, w_v)
            else:
                (hq,) = normed
            q = _matmul(hq, w_q, j, epilogue="headnorm", extra=q_norm[j], scale=HEAD_DIM ** -0.5)
            lam_vecs = jnp.stack([lam_q1[j], lam_k1[j], lam_q2[j], lam_k2[j]]).astype(F32)
            o = _diff_attention(q, k, v, rel_bias, lam_vecs, subln[j], batch=batch, seq=seq,
                                lambda_init=_lambda_init(layer))
            h, h_rows = _matmul(o, w_o, j, epilogue="residual", extra=h, out_dtype=F32)
        h, normed = _hier_moe(h, h_rows, m_norm[layer], m_w_group[layer], m_w_router[layer], w1, w3, w2, layer,
                              next_gains=pre_norm_gains(layer + 1))
    return h.reshape(batch, seq, d)
```

```python
import functools
import math

import jax
import jax.numpy as jnp
import numpy as np
from jax import lax
from jax.experimental import pallas as pl
from jax.experimental.pallas import tpu as pltpu

F32 = jnp.float32
BF16 = jnp.bfloat16

CHUNK = 128
HEAD_DIM = 128
N_BUCKETS = 32
MAX_DISTANCE = 128
TOP_K = 2
EPS = 1e-6
LN_EPS = 1e-5
NEG_INF = -1e30
FINITE_MAX = 3.0e38

LANES = 128
VMEM_LIMIT_BYTES = 56 * 1024 * 1024

NORM_TM = 512
MM_TM, MM_TM_NARROW_OUT, MM_TN = 512, 1024, 1024
GATE_TM = 512
ATTN_TQ, ATTN_TK = 2048, 512
ROUTER_TM = 512
MOE_TM = 256
COMBINE_TM = 256


def _params(semantics):
    return pltpu.CompilerParams(dimension_semantics=semantics, vmem_limit_bytes=VMEM_LIMIT_BYTES)


def _lambda_init(layer):
    return 0.8 - 0.6 * math.exp(-0.3 * layer)


def _rmsnorm_kernel(x_ref, g_ref, *o_refs):
    x = x_ref[...]
    y = x * lax.rsqrt(jnp.mean(x * x, axis=-1, keepdims=True) + EPS)
    for k, o_ref in enumerate(o_refs):
        o_ref[...] = (y * g_ref[k:k + 1, :]).astype(o_ref.dtype)


def _rmsnorm(x, gains):
    t, d = x.shape
    n = gains.shape[0]
    tm = min(NORM_TM, t)
    return pl.pallas_call(
        _rmsnorm_kernel,
        grid=(t // tm,),
        in_specs=[pl.BlockSpec((tm, d), lambda i: (i, 0)),
                  pl.BlockSpec((n, d), lambda i: (0, 0))],
        out_specs=[pl.BlockSpec((tm, d), lambda i: (i, 0))] * n,
        out_shape=[jax.ShapeDtypeStruct((t, d), BF16)] * n,
        name="rmsnorm",
        compiler_params=_params(("parallel",)),
    )(x, gains)


def _gelu_tanh(x):
    return 0.5 * x * (1.0 + jnp.tanh(math.sqrt(2.0 / math.pi) * (x + 0.044715 * (x * x * x))))


def _mm_kernel(a_ref, w_ref, *rest, epilogue, scale):
    wb_ref = rest[-1]
    o_ref = rest[1] if epilogue == "residual" else rest[-2]

    @pl.when(pl.program_id(1) == 0)
    def _():
        wb_ref[...] = w_ref[...].astype(BF16)

    acc = jnp.dot(a_ref[...], wb_ref[...], preferred_element_type=F32)
    if epilogue == "gelu_bias":
        o_ref[...] = _gelu_tanh(acc + rest[0][...]).astype(o_ref.dtype)
    elif epilogue == "residual":
        y = rest[0][...] + acc
        o_ref[...] = y
        rest[2][...] = y
    elif epilogue == "headnorm":
        gain = rest[0][...] * scale
        for g in range(acc.shape[1] // HEAD_DIM):
            y = acc[:, g * HEAD_DIM:(g + 1) * HEAD_DIM]
            y = y * lax.rsqrt(jnp.mean(y * y, axis=-1, keepdims=True) + EPS)
            o_ref[:, g * HEAD_DIM:(g + 1) * HEAD_DIM] = (y * gain).astype(o_ref.dtype)
    else:
        o_ref[...] = acc.astype(o_ref.dtype)


def _matmul(a, w, layer=0, *, epilogue="none", extra=None, scale=1.0, out_dtype=BF16):
    m, k = a.shape
    if w.ndim == 2:
        w = w.reshape(1, *w.shape)
    n = w.shape[2]
    tm, tn = min(MM_TM if out_dtype == F32 else MM_TM_NARROW_OUT, m), min(MM_TN, n)
    in_specs = [pl.BlockSpec((tm, k), lambda j, i: (i, 0)),
                pl.BlockSpec((None, k, tn), lambda j, i: (layer, 0, j), pipeline_mode=pl.Buffered(1))]
    args = [a, w]
    if epilogue == "gelu_bias":
        in_specs.append(pl.BlockSpec((1, tn), lambda j, i: (0, j)))
        args.append(extra.reshape(1, n))
    elif epilogue == "residual":
        in_specs.append(pl.BlockSpec((tm, tn), lambda j, i: (i, j)))
        args.append(extra)
    elif epilogue == "headnorm":
        in_specs.append(pl.BlockSpec((1, HEAD_DIM), lambda j, i: (0, 0)))
        args.append(extra.reshape(1, HEAD_DIM))
    out_specs = pl.BlockSpec((tm, tn), lambda j, i: (i, j))
    out_shape = jax.ShapeDtypeStruct((m, n), out_dtype)
    if epilogue == "residual":
        out_specs = [out_specs, pl.BlockSpec((tm, None, tn), lambda j, i: (i, 0, j))]
        out_shape = [out_shape, jax.ShapeDtypeStruct((m, 1, n), out_dtype)]
    return pl.pallas_call(
        functools.partial(_mm_kernel, epilogue=epilogue, scale=scale),
        grid=(n // tn, m // tm),
        in_specs=in_specs,
        out_specs=out_specs,
        out_shape=out_shape,
        scratch_shapes=[pltpu.VMEM((k, tn), BF16)],
        name="mm_" + epilogue,
        compiler_params=_params(("parallel", "arbitrary")),
    )(*args)


def _gmlp_gate_kernel(u_ref, v_ref, lng_ref, lnb_ref, ws_ref, bs_ref, o_ref, *, n_chunks, groups):
    v = v_ref[...].astype(F32)
    mu = jnp.mean(v, axis=-1, keepdims=True)
    vc = v - mu
    var = jnp.mean(vc * vc, axis=-1, keepdims=True)
    vn = (vc * lax.rsqrt(var + LN_EPS) * lng_ref[...] + lnb_ref[...]).astype(BF16)
    gw = v.shape[1] // groups
    row = lax.broadcasted_iota(jnp.int32, (CHUNK, CHUNK), 0)
    col = lax.broadcasted_iota(jnp.int32, (CHUNK, CHUNK), 1)
    causal = row >= col
    for g in range(groups):
        w = jnp.where(causal, ws_ref[g], 0.0).astype(BF16)
        b = bs_ref[g]
        for c in range(n_chunks):
            rows = slice(c * CHUNK, (c + 1) * CHUNK)
            cols = slice(g * gw, (g + 1) * gw)
            s = jnp.dot(w, vn[rows, cols], preferred_element_type=F32) + b
            o_ref[rows, cols] = (u_ref[rows, cols].astype(F32) * s).astype(o_ref.dtype)


def _gmlp_gate(z, ln_g, ln_b, w_s, b_s):
    t, d2 = z.shape
    dg = d2 // 2
    groups = w_s.shape[0]
    tm = min(GATE_TM, t)
    return pl.pallas_call(
        functools.partial(_gmlp_gate_kernel, n_chunks=tm // CHUNK, groups=groups),
        grid=(t // tm,),
        in_specs=[pl.BlockSpec((tm, dg), lambda i: (i, 0)),
                  pl.BlockSpec((tm, dg), lambda i: (i, 1)),
                  pl.BlockSpec((1, dg), lambda i: (0, 0)),
                  pl.BlockSpec((1, dg), lambda i: (0, 0)),
                  pl.BlockSpec((groups, CHUNK, CHUNK), lambda i: (0, 0, 0)),
                  pl.BlockSpec((groups, CHUNK, 1), lambda i: (0, 0, 0))],
        out_specs=pl.BlockSpec((tm, dg), lambda i: (i, 0)),
        out_shape=jax.ShapeDtypeStruct((t, dg), BF16),
        name="gmlp_gate",
        compiler_params=_params(("parallel",)),
    )(z, z, ln_g.reshape(1, dg), ln_b.reshape(1, dg), w_s, b_s.reshape(groups, CHUNK, 1))


def _t5_causal_bucket(rel):
    n = jnp.maximum(rel, 0)
    max_exact = N_BUCKETS // 2
    nf = jnp.maximum(n, 1).astype(F32)
    large = max_exact + (jnp.log(nf / max_exact) / math.log(MAX_DISTANCE / max_exact)
                         * (N_BUCKETS - max_exact)).astype(jnp.int32)
    large = jnp.minimum(large, N_BUCKETS - 1)
    return jnp.where(n < max_exact, n, large)


def _attn_kernel(lam_ref, qt_ref, k_ref, vt_ref, z_ref, subln_ref, o_ref, acc_ref, m_ref, l_ref, bias_ref,
                 *, tq, tk, lambda_init):
    qi = pl.program_id(2)
    ratio = tq // tk
    first_near = ratio * qi - 1

    @pl.when(qi == 0)
    def _():
        bias_ref[0] = pltpu.roll(jnp.broadcast_to(z_ref[0], (tk, 2 * tq)), 0, 1, stride=1, stride_axis=0)

    def first_query(near):
        return 0 if near is None else max(0, -near * tk)

    def row_pad(x, c0, fill):
        return x if c0 == 0 else jnp.concatenate([jnp.full((1, c0), fill, F32), x], axis=1)

    def scores(j, c, near):
        ks = pl.multiple_of(j * tk, tk)
        c0 = first_query(near)
        k = k_ref[0, pl.ds(ks, tk), c * HEAD_DIM:(c + 1) * HEAD_DIM]
        qt = qt_ref[0, c * HEAD_DIM:(c + 1) * HEAD_DIM, c0:]
        s = jnp.dot(k, qt, preferred_element_type=F32)
        if near is not None:
            d = near * tk
            lo = d + tq - tk + c0
            s = s + bias_ref[0, :, lo:lo + tq - c0]
            if d <= 0:
                valid = (lax.broadcasted_iota(jnp.int32, s.shape, 1) + (c0 + d)
                         >= lax.broadcasted_iota(jnp.int32, s.shape, 0))
                s = jnp.where(valid, s, NEG_INF)
        return s

    def fixed_step(j, near):
        vt = vt_ref[0, 0, j]
        c0 = first_query(near)
        for c in range(2):
            p = jnp.exp(scores(j, c, near) - m_ref[c][:, c0:])
            l_ref[c] += row_pad(jnp.sum(p, axis=0, keepdims=True), c0, 0.0)
            acc_ref[c, :, c0:] += jnp.dot(vt, p.astype(BF16), preferred_element_type=F32)

    def online_step(j, near):
        vt = vt_ref[0, 0, j]
        c0 = first_query(near)
        for c in range(2):
            s = scores(j, c, near)
            m_old = m_ref[c]
            m_prev = m_old[:, c0:]
            m_new = jnp.maximum(m_prev, jnp.max(s, axis=0, keepdims=True))
            alpha = jnp.exp(m_prev - m_new)
            p = jnp.exp(s - m_new)
            l_ref[c] = row_pad(alpha, c0, 1.0) * l_ref[c] + row_pad(jnp.sum(p, axis=0, keepdims=True), c0, 0.0)
            acc_ref[c, :, c0:] = (alpha * acc_ref[c, :, c0:]
                                  + jnp.dot(vt, p.astype(BF16), preferred_element_type=F32))
            m_ref[c] = m_new if c0 == 0 else jnp.concatenate([m_old[:, :c0], m_new], axis=1)

    def all_blocks(step):
        def far_body(j, carry):
            step(j, None)
            return carry
        lax.fori_loop(0, jnp.maximum(first_near, 0), far_body, 0)

        @pl.when(qi >= 1)
        def _():
            step(first_near, 1)

        for r in range(ratio):
            step(first_near + 1 + r, -r)

    l_ref[...] = jnp.zeros(l_ref.shape, F32)
    acc_ref[...] = jnp.zeros(acc_ref.shape, F32)
    self_bias = bias_ref[0, 0:1, tq - tk:tq - tk + 1]
    q0 = pl.multiple_of(qi * tq, tq)
    for c in range(2):
        k_self = k_ref[0, pl.ds(q0, tq), c * HEAD_DIM:(c + 1) * HEAD_DIM].astype(F32)
        qt = qt_ref[0, c * HEAD_DIM:(c + 1) * HEAD_DIM, :].astype(F32)
        m_ref[c] = jnp.sum(qt * k_self.T, axis=0, keepdims=True) + self_bias
    all_blocks(fixed_step)

    n_bad = (jnp.sum(jnp.where(jnp.abs(acc_ref[...]) < FINITE_MAX, 0.0, 1.0))
             + jnp.sum(jnp.where(jnp.abs(l_ref[...]) < FINITE_MAX, 0.0, 1.0)))

    @pl.when(n_bad > 0.0)
    def _():
        m_ref[...] = jnp.full(m_ref.shape, NEG_INF, F32)
        l_ref[...] = jnp.zeros(l_ref.shape, F32)
        acc_ref[...] = jnp.zeros(acc_ref.shape, F32)
        all_blocks(online_step)

    lam_v = lam_ref[...]
    lam = (jnp.exp(jnp.sum(lam_v[0:1] * lam_v[1:2], axis=-1, keepdims=True))
           - jnp.exp(jnp.sum(lam_v[2:3] * lam_v[3:4], axis=-1, keepdims=True)) + lambda_init)
    ot = acc_ref[0] / l_ref[0] - lam * (acc_ref[1] / l_ref[1])
    ot = ot * lax.rsqrt(jnp.mean(ot * ot, axis=0, keepdims=True) + EPS)
    ot = ot * subln_ref[...] * (1.0 - lambda_init)
    o_ref[...] = ot.T.astype(o_ref.dtype)


def _bias_by_distance(rel_bias, tq, tk):
    table = (rel_bias - rel_bias[N_BUCKETS - 1:N_BUCKETS, :]).astype(F32)
    rel = jnp.arange(2 * tq, dtype=jnp.int32) - (tq - tk)
    onehot = (_t5_causal_bucket(rel)[:, None] == jnp.arange(N_BUCKETS, dtype=jnp.int32)[None, :]).astype(F32)
    z = jnp.dot(onehot, table, precision=lax.Precision.HIGHEST).T
    return z.reshape(z.shape[0], 1, 2 * tq)


def _diff_attention(q, k, v, rel_bias, lam_vecs, subln, *, batch, seq, lambda_init):
    t, d = q.shape
    hd2 = 2 * HEAD_DIM
    n_heads = d // hd2
    tq = min(ATTN_TQ, seq)
    tk = min(ATTN_TK, tq)
    nq, nk = seq // tq, seq // tk
    assert tk >= CHUNK and tq % tk == 0 and seq % tq == 0
    far = np.arange(tk + 1, max(seq, tk + 2), dtype=np.int64)
    far_bucket = N_BUCKETS // 2 + (np.log(far.astype(np.float32) / (N_BUCKETS // 2))
                                   / math.log(MAX_DISTANCE / (N_BUCKETS // 2))
                                   * (N_BUCKETS - N_BUCKETS // 2)).astype(np.int64)
    assert np.all(far_bucket >= N_BUCKETS - 1)
    z = _bias_by_distance(rel_bias, tq, tk)
    qt = jnp.transpose(q.reshape(batch, seq, d), (0, 2, 1))
    k3 = k.reshape(batch, seq, d)
    vt = jnp.transpose(v.reshape(batch, nk, tk, n_heads, hd2), (0, 3, 1, 4, 2))
    return pl.pallas_call(
        functools.partial(_attn_kernel, tq=tq, tk=tk, lambda_init=lambda_init),
        grid=(batch, n_heads, nq),
        in_specs=[pl.BlockSpec((4, HEAD_DIM), lambda b, h, i: (0, 0)),
                  pl.BlockSpec((1, hd2, tq), lambda b, h, i: (b, h, i)),
                  pl.BlockSpec((1, seq, hd2), lambda b, h, i: (b, 0, h)),
                  pl.BlockSpec((1, 1, nk, hd2, tk), lambda b, h, i: (b, h, 0, 0, 0)),
                  pl.BlockSpec((1, 1, 2 * tq), lambda b, h, i: (h, 0, 0)),
                  pl.BlockSpec((hd2, 1), lambda b, h, i: (0, 0))],
        out_specs=pl.BlockSpec((tq, hd2), lambda b, h, i: (b * nq + i, h)),
        out_shape=jax.ShapeDtypeStruct((t, d), BF16),
        scratch_shapes=[pltpu.VMEM((2, hd2, tq), F32),
                        pltpu.VMEM((2, 1, tq), F32),
                        pltpu.VMEM((2, 1, tq), F32),
                        pltpu.VMEM((1, tk, 2 * tq), F32)],
        name="diff_attn",
        compiler_params=_params(("parallel", "parallel", "arbitrary")),
    )(lam_vecs, qt, k3, vt, z, subln.reshape(hd2, 1))


def _router_kernel(x_ref, g_ref, wr_ref, o_ref, cnt_ref, *, n_groups, n_experts):
    @pl.when(pl.program_id(0) == 0)
    def _():
        cnt_ref[...] = jnp.zeros(cnt_ref.shape, F32)

    x = x_ref[...]
    hn = x * lax.rsqrt(jnp.mean(x * x, axis=-1, keepdims=True) + EPS) * g_ref[...]
    hi = hn.astype(BF16)
    lo = (hn - hi.astype(F32)).astype(BF16)
    r1 = jnp.dot(hi, wr_ref[...], preferred_element_type=F32)
    r2 = jnp.dot(lo, wr_ref[:, :LANES], preferred_element_type=F32)
    logits = r1[:, :LANES] + r1[:, LANES:] + r2
    lane = lax.broadcasted_iota(jnp.int32, logits.shape, 1).astype(F32)
    no_lane = float(LANES)
    is_g = lane < n_groups
    gl = jnp.where(is_g, logits, -jnp.inf)
    gmax = jnp.max(gl, axis=-1, keepdims=True)
    gidx = jnp.min(jnp.where(gl == gmax, lane, no_lane), axis=-1, keepdims=True)
    gsum = jnp.sum(jnp.where(is_g, jnp.exp(gl - gmax), 0.0), axis=-1, keepdims=True)
    g_gate = 1.0 / gsum
    lo = n_groups + gidx * n_experts
    in_grp = (lane >= lo) & (lane < lo + n_experts)
    el = jnp.where(in_grp, logits, -jnp.inf)
    t1 = jnp.max(el, axis=-1, keepdims=True)
    i1 = jnp.min(jnp.where(el == t1, lane, no_lane), axis=-1, keepdims=True)
    el2 = jnp.where(lane == i1, -jnp.inf, el)
    t2 = jnp.max(el2, axis=-1, keepdims=True)
    i2 = jnp.min(jnp.where(el2 == t2, lane, no_lane), axis=-1, keepdims=True)
    e = jnp.exp(t2 - t1)
    w1 = g_gate / (1.0 + e)
    w2 = g_gate * e / (1.0 + e)
    e1 = i1 - n_groups
    e2 = i2 - n_groups
    tm = logits.shape[0]
    oh1 = lane == e1
    oh2 = lane == e2
    strict_lower = (lax.broadcasted_iota(jnp.int32, (tm, tm), 0)
                    > lax.broadcasted_iota(jnp.int32, (tm, tm), 1)).astype(BF16)
    pre1 = jnp.dot(strict_lower, oh1.astype(BF16), preferred_element_type=F32)
    pre2 = jnp.dot(strict_lower, oh2.astype(BF16), preferred_element_type=F32)
    tot1 = jnp.sum(oh1.astype(F32), axis=0, keepdims=True)
    tot2 = jnp.sum(oh2.astype(F32), axis=0, keepdims=True)
    base = cnt_ref[...]
    rank1 = jnp.sum(jnp.where(oh1, base + pre1, 0.0), axis=-1, keepdims=True)
    rank2 = jnp.sum(jnp.where(oh2, base + tot1 + pre2, 0.0), axis=-1, keepdims=True)
    cnt_ref[...] = base + tot1 + tot2
    out = jnp.zeros(logits.shape, F32)
    for k, val in enumerate((e1, e2, w1, w2, rank1, rank2)):
        out = jnp.where(lane == k, val, out)
    o_ref[...] = out


def _router(h, gain, w_group, w_router):
    t, d = h.shape
    n_groups, _, n_experts = w_router.shape
    tm = min(ROUTER_TM, t)
    assert n_groups + n_groups * n_experts <= LANES
    wr = jnp.concatenate([w_group, jnp.transpose(w_router, (1, 0, 2)).reshape(d, n_groups * n_experts)], axis=1)
    wr = jnp.pad(wr, ((0, 0), (0, LANES - wr.shape[1])))
    wr_hi = wr.astype(BF16)
    wr = jnp.concatenate([wr_hi, (wr - wr_hi.astype(F32)).astype(BF16)], axis=1)
    return pl.pallas_call(
        functools.partial(_router_kernel, n_groups=n_groups, n_experts=n_experts),
        grid=(t // tm,),
        in_specs=[pl.BlockSpec((tm, d), lambda i: (i, 0)),
                  pl.BlockSpec((1, d), lambda i: (0, 0)),
                  pl.BlockSpec((d, 2 * LANES), lambda i: (0, 0))],
        out_specs=[pl.BlockSpec((tm, LANES), lambda i: (i, 0)),
                   pl.BlockSpec((1, LANES), lambda i: (0, 0))],
        out_shape=[jax.ShapeDtypeStruct((t, LANES), F32),
                   jax.ShapeDtypeStruct((1, LANES), F32)],
        name="moe_router",
        compiler_params=_params(("arbitrary",)),
    )(h, gain.reshape(1, d), wr)


def _row_copy(src_hbm, idx, buf, slot, r, sem):
    src = src_hbm.at[idx] if len(src_hbm.shape) == 3 else src_hbm.at[pl.ds(idx, 1), :]
    return pltpu.make_async_copy(src, buf.at[slot, pl.ds(r, 1), :], sem.at[slot])


def _gather_start(src_hbm, idx_ref, base, buf, slot, sem, n_rows):
    def body(r, c):
        _row_copy(src_hbm, idx_ref[base + r], buf, slot, r, sem).start()
        return c
    lax.fori_loop(0, n_rows, body, 0, unroll=8)


def _gather_wait(src_hbm, buf, slot, sem, n_rows):
    def body(r, c):
        _row_copy(src_hbm, 0, buf, slot, r, sem).wait()
        return c
    lax.fori_loop(0, n_rows, body, 0, unroll=8)


def _expert_changed(te_ref, t):
    return (t == 0) | (te_ref[t] != te_ref[jnp.maximum(t - 1, 0)])


def _moe_up_kernel(te_ref, nv_ref, tok_ref, nxt_ref, h_hbm, g_ref, w1_hbm, w3_hbm, o_ref,
                   buf, sem, w1f_ref, w3f_ref, wsem, w1b_ref, w3b_ref, *, tm, e0):
    t = pl.program_id(0)
    n_valid = nv_ref[0]

    def weight_copies(e):
        return (pltpu.make_async_copy(w1_hbm.at[e0 + e], w1f_ref, wsem.at[0]),
                pltpu.make_async_copy(w3_hbm.at[e0 + e], w3f_ref, wsem.at[1]))

    @pl.when(t == 0)
    def _():
        for cp in weight_copies(te_ref[0]):
            cp.start()
        _gather_start(h_hbm, tok_ref, 0, buf, 0, sem, tm)

    @pl.when(_expert_changed(te_ref, t))
    def _():
        for cp in weight_copies(te_ref[t]):
            cp.wait()
        w1b_ref[...] = w1f_ref[...].astype(BF16)
        w3b_ref[...] = w3f_ref[...].astype(BF16)

        @pl.when(nxt_ref[t] >= 0)
        def _():
            for cp in weight_copies(nxt_ref[t]):
                cp.start()

    def step(slot):
        _gather_wait(h_hbm, buf, slot, sem, tm)
        base = jnp.minimum(t + 1, n_valid - 1) * tm
        for r in range(tm):
            _row_copy(h_hbm, tok_ref[base + r], buf, 1 - slot, r, sem).start()
        x = buf[slot]
        hn = (x * lax.rsqrt(jnp.mean(x * x, axis=-1, keepdims=True) + EPS) * g_ref[...]).astype(BF16)
        a = jnp.dot(hn, w1b_ref[...], preferred_element_type=F32)
        b = jnp.dot(hn, w3b_ref[...], preferred_element_type=F32)
        o_ref[...] = (a * jax.nn.sigmoid(a) * b).astype(o_ref.dtype)

        @pl.when(t == n_valid - 1)
        def _():
            _gather_wait(h_hbm, buf, 1 - slot, sem, tm)

    for slot in range(2):
        @pl.when((t < n_valid) & (t % 2 == slot))
        def _():
            step(slot)

    @pl.when(t >= n_valid)
    def _():
        o_ref[...] = jnp.zeros(o_ref.shape, o_ref.dtype)


def _moe_down_kernel(te_ref, nv_ref, h_ref, w2_ref, o_ref, w2b_ref):
    t = pl.program_id(0)

    @pl.when(_expert_changed(te_ref, t))
    def _():
        w2b_ref[...] = w2_ref[0].astype(BF16)

    @pl.when(t < nv_ref[0])
    def _():
        o_ref[...] = jnp.dot(h_ref[...], w2b_ref[...], preferred_element_type=F32)

    @pl.when(t >= nv_ref[0])
    def _():
        o_ref[...] = jnp.zeros(o_ref.shape, o_ref.dtype)


def _combine_kernel(pos_ref, y_hbm, h_ref, route_ref, *rest, tm, n_tok, n_norms):
    g_ref = rest[0] if n_norms else None
    o_ref = rest[1 if n_norms else 0]
    norm_refs = rest[2:2 + n_norms] if n_norms else ()
    buf0, buf1, sem0, sem1 = rest[-4:]
    t = pl.program_id(0)
    n = pl.num_programs(0)
    slot = t % 2

    def start(step, s):
        _gather_start(y_hbm, pos_ref, step * tm, buf0, s, sem0, tm)
        _gather_start(y_hbm, pos_ref, n_tok + step * tm, buf1, s, sem1, tm)

    @pl.when(t == 0)
    def _():
        start(0, 0)

    @pl.when(t + 1 < n)
    def _():
        start(t + 1, 1 - slot)

    _gather_wait(y_hbm, buf0, slot, sem0, tm)
    _gather_wait(y_hbm, buf1, slot, sem1, tm)
    w = route_ref[...]
    y = h_ref[...] + w[:, 2:3] * buf0[slot] + w[:, 3:4] * buf1[slot]
    o_ref[...] = y
    if n_norms:
        yn = y * lax.rsqrt(jnp.mean(y * y, axis=-1, keepdims=True) + EPS)
        for k, n_ref in enumerate(norm_refs):
            n_ref[...] = (yn * g_ref[k:k + 1, :]).astype(n_ref.dtype)


def _hier_moe(h, h_rows, gain, w_group, w_router, w1, w3, w2, layer, next_gains=None):
    t, d = h.shape
    f = w1.shape[2]
    ne = w_router.shape[0] * w_router.shape[2]
    e0 = layer * ne
    tm, tc = MOE_TM, COMBINE_TM
    route, counts = _router(h, gain, w_group, w_router)
    e_pair = jnp.concatenate([route[:, 0], route[:, 1]]).astype(jnp.int32)
    rank = jnp.concatenate([route[:, 4], route[:, 5]]).astype(jnp.int32)

    counts = counts[0, :ne].astype(jnp.int32)
    tiles_e = (counts + tm - 1) // tm
    tile_end = jnp.cumsum(tiles_e)
    row_start = (tile_end - tiles_e) * tm
    is_e = e_pair[:, None] == jnp.arange(ne, dtype=jnp.int32)[None, :]
    dest = jnp.sum(jnp.where(is_e, row_start[None, :], 0), axis=1) + rank
    n_tiles = (TOP_K * t) // tm + ne
    n_rows = n_tiles * tm
    dest = dest.astype(jnp.int32)
    tok_pair = jnp.tile(jnp.arange(t, dtype=jnp.int32), TOP_K)
    tok_sorted = jnp.zeros((n_rows,), jnp.int32).at[dest].set(tok_pair)
    n_valid = tile_end[-1]
    tile_ids = jnp.minimum(jnp.arange(n_tiles, dtype=jnp.int32), n_valid - 1)
    tile_expert = jnp.sum((tile_end[None, :] <= tile_ids[:, None]).astype(jnp.int32), axis=1)
    run_end = jnp.sum(jnp.where(tile_expert[:, None] == jnp.arange(ne, dtype=jnp.int32)[None, :],
                                tile_end[None, :], 0), axis=1)
    follows = run_end[:, None] == jnp.arange(n_tiles, dtype=jnp.int32)[None, :]
    next_expert = jnp.where(run_end < n_valid, jnp.sum(jnp.where(follows, tile_expert[None, :], 0), axis=1), -1)
    n_valid = n_valid.reshape(1).astype(jnp.int32)

    hid = pl.pallas_call(
        functools.partial(_moe_up_kernel, tm=tm, e0=e0),
        grid_spec=pltpu.PrefetchScalarGridSpec(
            num_scalar_prefetch=4,
            grid=(n_tiles,),
            in_specs=[pl.BlockSpec(memory_space=pl.ANY),
                      pl.BlockSpec((1, d), lambda i, te, nv, tok, nxt: (0, 0)),
                      pl.BlockSpec(memory_space=pl.ANY),
                      pl.BlockSpec(memory_space=pl.ANY)],
            out_specs=pl.BlockSpec((tm, f), lambda i, te, nv, tok, nxt: (i, 0)),
            scratch_shapes=[pltpu.VMEM((2, tm, d), F32), pltpu.SemaphoreType.DMA((2,)),
                            pltpu.VMEM((d, f), F32), pltpu.VMEM((d, f), F32), pltpu.SemaphoreType.DMA((2,)),
                            pltpu.VMEM((d, f), BF16), pltpu.VMEM((d, f), BF16)]),
        out_shape=jax.ShapeDtypeStruct((n_rows, f), BF16),
        name="moe_up",
        compiler_params=_params(("arbitrary",)),
    )(tile_expert, n_valid, tok_sorted, next_expert.astype(jnp.int32), h_rows, gain.reshape(1, d), w1, w3)

    ys = pl.pallas_call(
        _moe_down_kernel,
        grid_spec=pltpu.PrefetchScalarGridSpec(
            num_scalar_prefetch=2,
            grid=(n_tiles,),
            in_specs=[pl.BlockSpec((tm, f), lambda i, te, nv: (i, 0)),
                      pl.BlockSpec((1, f, d), lambda i, te, nv: (e0 + te[i], 0, 0))],
            out_specs=pl.BlockSpec((tm, d), lambda i, te, nv: (i, 0)),
            scratch_shapes=[pltpu.VMEM((f, d), BF16)]),
        out_shape=jax.ShapeDtypeStruct((n_rows, d), F32),
        name="moe_down",
        compiler_params=_params(("arbitrary",)),
    )(tile_expert, n_valid, hid, w2)

    n_norms = 0 if next_gains is None else next_gains.shape[0]
    row_spec = pl.BlockSpec((tc, d), lambda i, pos: (i, 0))
    in_specs = [pl.BlockSpec(memory_space=pl.ANY), row_spec, pl.BlockSpec((tc, LANES), lambda i, pos: (i, 0))]
    args = [dest.astype(jnp.int32), ys, h, route]
    if n_norms:
        in_specs.append(pl.BlockSpec((n_norms, d), lambda i, pos: (0, 0)))
        args.append(next_gains)
    outs = pl.pallas_call(
        functools.partial(_combine_kernel, tm=tc, n_tok=t, n_norms=n_norms),
        grid_spec=pltpu.PrefetchScalarGridSpec(
            num_scalar_prefetch=1,
            grid=(t // tc,),
            in_specs=in_specs,
            out_specs=[row_spec] * (1 + n_norms),
            scratch_shapes=[pltpu.VMEM((2, tc, d), F32), pltpu.VMEM((2, tc, d), F32),
                            pltpu.SemaphoreType.DMA((2,)), pltpu.SemaphoreType.DMA((2,))]),
        out_shape=[jax.ShapeDtypeStruct((t, d), F32)] + [jax.ShapeDtypeStruct((t, d), BF16)] * n_norms,
        name="moe_combine",
        compiler_params=_params(("arbitrary",)),
    )(*args)
    return outs[0], tuple(outs[1:])


def kernel(x, a_norm, a_w_in, a_b_in, a_ln_g, a_ln_b, a_w_s, a_b_s, a_w_out, kv_norm, w_k, w_v, k_norm, b_norm, w_q, q_norm, lam_q1, lam_k1, lam_q2, lam_k2, subln, w_o, rel_bias, m_norm, m_w_group, m_w_router, m_w1, m_w3, m_w2):
    batch, seq, d = x.shape
    depth = m_norm.shape[0]
    n_a = a_norm.shape[0]
    h = x.reshape(batch * seq, d)
    f = m_w1.shape[-1]
    w1 = m_w1.reshape(-1, d, f)
    w3 = m_w3.reshape(-1, d, f)
    w2 = m_w2.reshape(-1, f, d)
    def pre_norm_gains(layer):
        if layer >= depth:
            return None
        if layer < n_a:
            return a_norm[layer:layer + 1]
        j = layer - n_a
        return jnp.stack([kv_norm, b_norm[j]]) if layer == n_a else b_norm[j:j + 1]

    k = v = None
    normed = _rmsnorm(h, pre_norm_gains(0))
    for layer in range(depth):
        if layer < n_a:
            i = layer
            (hn,) = normed
            z = _matmul(hn, a_w_in, i, epilogue="gelu_bias", extra=a_b_in[i])
            gz = _gmlp_gate(z, a_ln_g[i], a_ln_b[i], a_w_s[i], a_b_s[i])
            h, h_rows = _matmul(gz, a_w_out, i, epilogue="residual", extra=h, out_dtype=F32)
        else:
            j = layer - n_a
            if layer == n_a:
                hkv, hq = normed
                k = _matmul(hkv, w_k, epilogue="headnorm", extra=k_norm)
                v = _matmul(hkv, w_v)
            else:
                (hq,) = normed
            q = _matmul(hq, w_q, j, epilogue="headnorm", extra=q_norm[j], scale=HEAD_DIM ** -0.5)
            lam_vecs = jnp.stack([lam_q1[j], lam_k1[j], lam_q2[j], lam_k2[j]]).astype(F32)
            o = _diff_attention(q, k, v, rel_bias, lam_vecs, subln[j], batch=batch, seq=seq,
                                lambda_init=_lambda_init(layer))
            h, h_rows = _matmul(o, w_o, j, epilogue="residual", extra=h, out_dtype=F32)
        h, normed = _hier_moe(h, h_rows, m_norm[layer], m_w_group[layer], m_w_router[layer], w1, w3, w2, layer,
                              next_gains=pre_norm_gains(layer + 1))
    return h.reshape(batch, seq, d)
```

```python
import functools
import math

import jax
import jax.numpy as jnp
import numpy as np
from jax import lax
from jax.experimental import pallas as pl
from jax.experimental.pallas import tpu as pltpu

F32 = jnp.float32
BF16 = jnp.bfloat16

CHUNK = 128
HEAD_DIM = 128
N_BUCKETS = 32
MAX_DISTANCE = 128
FAR_DISTANCE = MAX_DISTANCE
TOP_K = 2
EPS = 1e-6
LN_EPS = 1e-5
NEG_INF = -1e30
FINITE_MAX = 3.0e38

LANES = 128
VMEM_LIMIT_BYTES = 56 * 1024 * 1024

NORM_TM = 512
MM_TM, MM_TM_NARROW_OUT, MM_TN = 512, 1024, 1024
GATE_TM = 512
ATTN_TQ, ATTN_TK = 2048, 512
ROUTER_TM = 512
MOE_TM = 256
COMBINE_TM = 256


def _params(semantics):
    return pltpu.CompilerParams(dimension_semantics=semantics, vmem_limit_bytes=VMEM_LIMIT_BYTES)


def _lambda_init(layer):
    return 0.8 - 0.6 * math.exp(-0.3 * layer)


def _rmsnorm_kernel(x_ref, g_ref, *o_refs):
    x = x_ref[...]
    y = x * lax.rsqrt(jnp.mean(x * x, axis=-1, keepdims=True) + EPS)
    for k, o_ref in enumerate(o_refs):
        o_ref[...] = (y * g_ref[k:k + 1, :]).astype(o_ref.dtype)


def _rmsnorm(x, gains):
    t, d = x.shape
    n = gains.shape[0]
    tm = min(NORM_TM, t)
    return pl.pallas_call(
        _rmsnorm_kernel,
        grid=(t // tm,),
        in_specs=[pl.BlockSpec((tm, d), lambda i: (i, 0)),
                  pl.BlockSpec((n, d), lambda i: (0, 0))],
        out_specs=[pl.BlockSpec((tm, d), lambda i: (i, 0))] * n,
        out_shape=[jax.ShapeDtypeStruct((t, d), BF16)] * n,
        name="rmsnorm",
        compiler_params=_params(("parallel",)),
    )(x, gains)


def _gelu_tanh(x):
    return 0.5 * x * (1.0 + jnp.tanh(math.sqrt(2.0 / math.pi) * (x + 0.044715 * (x * x * x))))


def _mm_kernel(a_ref, w_ref, *rest, epilogue, scale):
    wb_ref = rest[-1]
    o_ref = rest[1] if epilogue == "residual" else rest[-2]

    @pl.when(pl.program_id(1) == 0)
    def _():
        wb_ref[...] = w_ref[...].astype(BF16)

    acc = jnp.dot(a_ref[...], wb_ref[...], preferred_element_type=F32)
    if epilogue == "gelu_bias":
        o_ref[...] = _gelu_tanh(acc + rest[0][...]).astype(o_ref.dtype)
    elif epilogue == "residual":
        y = rest[0][...] + acc
        o_ref[...] = y
        rest[2][...] = y
    elif epilogue == "headnorm":
        gain = rest[0][...] * scale
        for g in range(acc.shape[1] // HEAD_DIM):
            y = acc[:, g * HEAD_DIM:(g + 1) * HEAD_DIM]
            y = y * lax.rsqrt(jnp.mean(y * y, axis=-1, keepdims=True) + EPS)
            o_ref[:, g * HEAD_DIM:(g + 1) * HEAD_DIM] = (y * gain).astype(o_ref.dtype)
    else:
        o_ref[...] = acc.astype(o_ref.dtype)


def _matmul(a, w, layer=0, *, epilogue="none", extra=None, scale=1.0, out_dtype=BF16):
    m, k = a.shape
    if w.ndim == 2:
        w = w.reshape(1, *w.shape)
    n = w.shape[2]
    tm, tn = min(MM_TM if out_dtype == F32 else MM_TM_NARROW_OUT, m), min(MM_TN, n)
    in_specs = [pl.BlockSpec((tm, k), lambda j, i: (i, 0)),
                pl.BlockSpec((None, k, tn), lambda j, i: (layer, 0, j), pipeline_mode=pl.Buffered(1))]
    args = [a, w]
    if epilogue == "gelu_bias":
        in_specs.append(pl.BlockSpec((1, tn), lambda j, i: (0, j)))
        args.append(extra.reshape(1, n))
    elif epilogue == "residual":
        in_specs.append(pl.BlockSpec((tm, tn), lambda j, i: (i, j)))
        args.append(extra)
    elif epilogue == "headnorm":
        in_specs.append(pl.BlockSpec((1, HEAD_DIM), lambda j, i: (0, 0)))
        args.append(extra.reshape(1, HEAD_DIM))
    out_specs = pl.BlockSpec((tm, tn), lambda j, i: (i, j))
    out_shape = jax.ShapeDtypeStruct((m, n), out_dtype)
    if epilogue == "residual":
        out_specs = [out_specs, pl.BlockSpec((tm, None, tn), lambda j, i: (i, 0, j))]
        out_shape = [out_shape, jax.ShapeDtypeStruct((m, 1, n), out_dtype)]
    return pl.pallas_call(
        functools.partial(_mm_kernel, epilogue=epilogue, scale=scale),
        grid=(n // tn, m // tm),
        in_specs=in_specs,
        out_specs=out_specs,
        out_shape=out_shape,
        scratch_shapes=[pltpu.VMEM((k, tn), BF16)],
        name="mm_" + epilogue,
        compiler_params=_params(("parallel", "arbitrary")),
    )(*args)


def _gmlp_gate_kernel(u_ref, v_ref, lng_ref, lnb_ref, ws_ref, bs_ref, o_ref, *, n_chunks, groups):
    v = v_ref[...].astype(F32)
    mu = jnp.mean(v, axis=-1, keepdims=True)
    vc = v - mu
    var = jnp.mean(vc * vc, axis=-1, keepdims=True)
    vn = (vc * lax.rsqrt(var + LN_EPS) * lng_ref[...] + lnb_ref[...]).astype(BF16)
    gw = v.shape[1] // groups
    row = lax.broadcasted_iota(jnp.int32, (CHUNK, CHUNK), 0)
    col = lax.broadcasted_iota(jnp.int32, (CHUNK, CHUNK), 1)
    causal = row >= col
    for g in range(groups):
        w = jnp.where(causal, ws_ref[g], 0.0).astype(BF16)
        b = bs_ref[g]
        for c in range(n_chunks):
            rows = slice(c * CHUNK, (c + 1) * CHUNK)
            cols = slice(g * gw, (g + 1) * gw)
            s = jnp.dot(w, vn[rows, cols], preferred_element_type=F32) + b
            o_ref[rows, cols] = (u_ref[rows, cols].astype(F32) * s).astype(o_ref.dtype)


def _gmlp_gate(z, ln_g, ln_b, w_s, b_s):
    t, d2 = z.shape
    dg = d2 // 2
    groups = w_s.shape[0]
    tm = min(GATE_TM, t)
    return pl.pallas_call(
        functools.partial(_gmlp_gate_kernel, n_chunks=tm // CHUNK, groups=groups),
        grid=(t // tm,),
        in_specs=[pl.BlockSpec((tm, dg), lambda i: (i, 0)),
                  pl.BlockSpec((tm, dg), lambda i: (i, 1)),
                  pl.BlockSpec((1, dg), lambda i: (0, 0)),
                  pl.BlockSpec((1, dg), lambda i: (0, 0)),
                  pl.BlockSpec((groups, CHUNK, CHUNK), lambda i: (0, 0, 0)),
                  pl.BlockSpec((groups, CHUNK, 1), lambda i: (0, 0, 0))],
        out_specs=pl.BlockSpec((tm, dg), lambda i: (i, 0)),
        out_shape=jax.ShapeDtypeStruct((t, dg), BF16),
        name="gmlp_gate",
        compiler_params=_params(("parallel",)),
    )(z, z, ln_g.reshape(1, dg), ln_b.reshape(1, dg), w_s, b_s.reshape(groups, CHUNK, 1))


def _t5_causal_bucket(rel):
    n = jnp.maximum(rel, 0)
    max_exact = N_BUCKETS // 2
    nf = jnp.maximum(n, 1).astype(F32)
    large = max_exact + (jnp.log(nf / max_exact) / math.log(MAX_DISTANCE / max_exact)
                         * (N_BUCKETS - max_exact)).astype(jnp.int32)
    large = jnp.minimum(large, N_BUCKETS - 1)
    return jnp.where(n < max_exact, n, large)


def _attn_kernel(lam_ref, qt_ref, k_ref, vt_ref, z_ref, subln_ref, o_ref, acc_ref, m_ref, l_ref, bias_ref,
                 *, tq, tk, lambda_init):
    qi = pl.program_id(2)
    ratio = tq // tk
    first_near = ratio * qi - 1

    @pl.when(qi == 0)
    def _():
        bias_ref[0] = pltpu.roll(jnp.broadcast_to(z_ref[0], (tk, 2 * tq)), 0, 1, stride=1, stride_axis=0)

    def first_query(near):
        return 0 if near is None else max(0, -near * tk)

    def row_pad(x, c0, fill):
        return x if c0 == 0 else jnp.concatenate([jnp.full((1, c0), fill, F32), x], axis=1)

    def scores(j, c, near):
        ks = pl.multiple_of(j * tk, tk)
        c0 = first_query(near)
        k = k_ref[0, pl.ds(ks, tk), c * HEAD_DIM:(c + 1) * HEAD_DIM]
        qt = qt_ref[0, c * HEAD_DIM:(c + 1) * HEAD_DIM, c0:]
        s = jnp.dot(k, qt, preferred_element_type=F32)
        if near is not None:
            d = near * tk
            lo = d + tq - tk + c0
            nb = min(tq - c0, (tk if d <= 0 else 0) + FAR_DISTANCE)
            sn = s[:, :nb] + bias_ref[0, :, lo:lo + nb]
            if d <= 0:
                valid = (lax.broadcasted_iota(jnp.int32, sn.shape, 1) + (c0 + d)
                         >= lax.broadcasted_iota(jnp.int32, sn.shape, 0))
                sn = jnp.where(valid, sn, NEG_INF)
            s = sn if nb == tq - c0 else jnp.concatenate([sn, s[:, nb:]], axis=1)
        return s

    def fixed_step(j, near):
        vt = vt_ref[0, 0, j]
        c0 = first_query(near)
        for c in range(2):
            p = jnp.exp(scores(j, c, near) - m_ref[c][:, c0:])
            l_ref[c] += row_pad(jnp.sum(p, axis=0, keepdims=True), c0, 0.0)
            acc_ref[c, :, c0:] += jnp.dot(vt, p.astype(BF16), preferred_element_type=F32)

    def online_step(j, near):
        vt = vt_ref[0, 0, j]
        c0 = first_query(near)
        for c in range(2):
            s = scores(j, c, near)
            m_old = m_ref[c]
            m_prev = m_old[:, c0:]
            m_new = jnp.maximum(m_prev, jnp.max(s, axis=0, keepdims=True))
            alpha = jnp.exp(m_prev - m_new)
            p = jnp.exp(s - m_new)
            l_ref[c] = row_pad(alpha, c0, 1.0) * l_ref[c] + row_pad(jnp.sum(p, axis=0, keepdims=True), c0, 0.0)
            acc_ref[c, :, c0:] = (alpha * acc_ref[c, :, c0:]
                                  + jnp.dot(vt, p.astype(BF16), preferred_element_type=F32))
            m_ref[c] = m_new if c0 == 0 else jnp.concatenate([m_old[:, :c0], m_new], axis=1)

    def all_blocks(step):
        def far_body(j, carry):
            step(j, None)
            return carry
        lax.fori_loop(0, jnp.maximum(first_near, 0), far_body, 0)

        @pl.when(qi >= 1)
        def _():
            step(first_near, 1)

        for r in range(ratio):
            step(first_near + 1 + r, -r)

    l_ref[...] = jnp.zeros(l_ref.shape, F32)
    acc_ref[...] = jnp.zeros(acc_ref.shape, F32)
    self_bias = bias_ref[0, 0:1, tq - tk:tq - tk + 1]
    q0 = pl.multiple_of(qi * tq, tq)
    for c in range(2):
        k_self = k_ref[0, pl.ds(q0, tq), c * HEAD_DIM:(c + 1) * HEAD_DIM].astype(F32)
        qt = qt_ref[0, c * HEAD_DIM:(c + 1) * HEAD_DIM, :].astype(F32)
        m_ref[c] = jnp.sum(qt * k_self.T, axis=0, keepdims=True) + self_bias
    all_blocks(fixed_step)

    n_bad = (jnp.sum(jnp.where(jnp.abs(acc_ref[...]) < FINITE_MAX, 0.0, 1.0))
             + jnp.sum(jnp.where(jnp.abs(l_ref[...]) < FINITE_MAX, 0.0, 1.0)))

    @pl.when(n_bad > 0.0)
    def _():
        m_ref[...] = jnp.full(m_ref.shape, NEG_INF, F32)
        l_ref[...] = jnp.zeros(l_ref.shape, F32)
        acc_ref[...] = jnp.zeros(acc_ref.shape, F32)
        all_blocks(online_step)

    lam_v = lam_ref[...]
    lam = (jnp.exp(jnp.sum(lam_v[0:1] * lam_v[1:2], axis=-1, keepdims=True))
           - jnp.exp(jnp.sum(lam_v[2:3] * lam_v[3:4], axis=-1, keepdims=True)) + lambda_init)
    ot = acc_ref[0] / l_ref[0] - lam * (acc_ref[1] / l_ref[1])
    ot = ot * lax.rsqrt(jnp.mean(ot * ot, axis=0, keepdims=True) + EPS)
    ot = ot * subln_ref[...] * (1.0 - lambda_init)
    o_ref[...] = ot.T.astype(o_ref.dtype)


def _bias_by_distance(rel_bias, tq, tk):
    table = (rel_bias - rel_bias[N_BUCKETS - 1:N_BUCKETS, :]).astype(F32)
    rel = jnp.arange(2 * tq, dtype=jnp.int32) - (tq - tk)
    onehot = (_t5_causal_bucket(rel)[:, None] == jnp.arange(N_BUCKETS, dtype=jnp.int32)[None, :]).astype(F32)
    z = jnp.dot(onehot, table, precision=lax.Precision.HIGHEST).T
    return z.reshape(z.shape[0], 1, 2 * tq)


def _diff_attention(q, k, v, rel_bias, lam_vecs, subln, *, batch, seq, lambda_init):
    t, d = q.shape
    hd2 = 2 * HEAD_DIM
    n_heads = d // hd2
    tq = min(ATTN_TQ, seq)
    tk = min(ATTN_TK, tq)
    nq, nk = seq // tq, seq // tk
    assert tk >= FAR_DISTANCE and tq % tk == 0 and seq % tq == 0
    far = np.arange(FAR_DISTANCE + 1, max(seq, FAR_DISTANCE + 2), dtype=np.int64)
    far_bucket = N_BUCKETS // 2 + (np.log(far.astype(np.float32) / (N_BUCKETS // 2))
                                   / math.log(MAX_DISTANCE / (N_BUCKETS // 2))
                                   * (N_BUCKETS - N_BUCKETS // 2)).astype(np.int64)
    assert np.all(far_bucket >= N_BUCKETS - 1)
    z = _bias_by_distance(rel_bias, tq, tk)
    qt = jnp.transpose(q.reshape(batch, seq, d), (0, 2, 1))
    k3 = k.reshape(batch, seq, d)
    vt = jnp.transpose(v.reshape(batch, nk, tk, n_heads, hd2), (0, 3, 1, 4, 2))
    return pl.pallas_call(
        functools.partial(_attn_kernel, tq=tq, tk=tk, lambda_init=lambda_init),
        grid=(batch, n_heads, nq),
        in_specs=[pl.BlockSpec((4, HEAD_DIM), lambda b, h, i: (0, 0)),
                  pl.BlockSpec((1, hd2, tq), lambda b, h, i: (b, h, i)),
                  pl.BlockSpec((1, seq, hd2), lambda b, h, i: (b, 0, h)),
                  pl.BlockSpec((1, 1, nk, hd2, tk), lambda b, h, i: (b, h, 0, 0, 0)),
                  pl.BlockSpec((1, 1, 2 * tq), lambda b, h, i: (h, 0, 0)),
                  pl.BlockSpec((hd2, 1), lambda b, h, i: (0, 0))],
        out_specs=pl.BlockSpec((tq, hd2), lambda b, h, i: (b * nq + i, h)),
        out_shape=jax.ShapeDtypeStruct((t, d), BF16),
        scratch_shapes=[pltpu.VMEM((2, hd2, tq), F32),
                        pltpu.VMEM((2, 1, tq), F32),
                        pltpu.VMEM((2, 1, tq), F32),
                        pltpu.VMEM((1, tk, 2 * tq), F32)],
        name="diff_attn",
        compiler_params=_params(("parallel", "parallel", "arbitrary")),
    )(lam_vecs, qt, k3, vt, z, subln.reshape(hd2, 1))


def _router_kernel(x_ref, g_ref, wr_ref, o_ref, cnt_ref, *, n_groups, n_experts):
    @pl.when(pl.program_id(0) == 0)
    def _():
        cnt_ref[...] = jnp.zeros(cnt_ref.shape, F32)

    x = x_ref[...]
    hn = x * lax.rsqrt(jnp.mean(x * x, axis=-1, keepdims=True) + EPS) * g_ref[...]
    hi = hn.astype(BF16)
    lo = (hn - hi.astype(F32)).astype(BF16)
    r1 = jnp.dot(hi, wr_ref[...], preferred_element_type=F32)
    r2 = jnp.dot(lo, wr_ref[:, :LANES], preferred_element_type=F32)
    logits = r1[:, :LANES] + r1[:, LANES:] + r2
    lane = lax.broadcasted_iota(jnp.int32, logits.shape, 1).astype(F32)
    no_lane = float(LANES)
    is_g = lane < n_groups
    gl = jnp.where(is_g, logits, -jnp.inf)
    gmax = jnp.max(gl, axis=-1, keepdims=True)
    gidx = jnp.min(jnp.where(gl == gmax, lane, no_lane), axis=-1, keepdims=True)
    gsum = jnp.sum(jnp.where(is_g, jnp.exp(gl - gmax), 0.0), axis=-1, keepdims=True)
    g_gate = 1.0 / gsum
    lo = n_groups + gidx * n_experts
    in_grp = (lane >= lo) & (lane < lo + n_experts)
    el = jnp.where(in_grp, logits, -jnp.inf)
    t1 = jnp.max(el, axis=-1, keepdims=True)
    i1 = jnp.min(jnp.where(el == t1, lane, no_lane), axis=-1, keepdims=True)
    el2 = jnp.where(lane == i1, -jnp.inf, el)
    t2 = jnp.max(el2, axis=-1, keepdims=True)
    i2 = jnp.min(jnp.where(el2 == t2, lane, no_lane), axis=-1, keepdims=True)
    e = jnp.exp(t2 - t1)
    w1 = g_gate / (1.0 + e)
    w2 = g_gate * e / (1.0 + e)
    e1 = i1 - n_groups
    e2 = i2 - n_groups
    tm = logits.shape[0]
    oh1 = lane == e1
    oh2 = lane == e2
    strict_lower = (lax.broadcasted_iota(jnp.int32, (tm, tm), 0)
                    > lax.broadcasted_iota(jnp.int32, (tm, tm), 1)).astype(BF16)
    pre1 = jnp.dot(strict_lower, oh1.astype(BF16), preferred_element_type=F32)
    pre2 = jnp.dot(strict_lower, oh2.astype(BF16), preferred_element_type=F32)
    tot1 = jnp.sum(oh1.astype(F32), axis=0, keepdims=True)
    tot2 = jnp.sum(oh2.astype(F32), axis=0, keepdims=True)
    base = cnt_ref[...]
    rank1 = jnp.sum(jnp.where(oh1, base + pre1, 0.0), axis=-1, keepdims=True)
    rank2 = jnp.sum(jnp.where(oh2, base + tot1 + pre2, 0.0), axis=-1, keepdims=True)
    cnt_ref[...] = base + tot1 + tot2
    out = jnp.zeros(logits.shape, F32)
    for k, val in enumerate((e1, e2, w1, w2, rank1, rank2)):
        out = jnp.where(lane == k, val, out)
    o_ref[...] = out


def _router(h, gain, w_group, w_router):
    t, d = h.shape
    n_groups, _, n_experts = w_router.shape
    tm = min(ROUTER_TM, t)
    assert n_groups + n_groups * n_experts <= LANES
    wr = jnp.concatenate([w_group, jnp.transpose(w_router, (1, 0, 2)).reshape(d, n_groups * n_experts)], axis=1)
    wr = jnp.pad(wr, ((0, 0), (0, LANES - wr.shape[1])))
    wr_hi = wr.astype(BF16)
    wr = jnp.concatenate([wr_hi, (wr - wr_hi.astype(F32)).astype(BF16)], axis=1)
    return pl.pallas_call(
        functools.partial(_router_kernel, n_groups=n_groups, n_experts=n_experts),
        grid=(t // tm,),
        in_specs=[pl.BlockSpec((tm, d), lambda i: (i, 0)),
                  pl.BlockSpec((1, d), lambda i: (0, 0)),
                  pl.BlockSpec((d, 2 * LANES), lambda i: (0, 0))],
        out_specs=[pl.BlockSpec((tm, LANES), lambda i: (i, 0)),
                   pl.BlockSpec((1, LANES), lambda i: (0, 0))],
        out_shape=[jax.ShapeDtypeStruct((t, LANES), F32),
                   jax.ShapeDtypeStruct((1, LANES), F32)],
        name="moe_router",
        compiler_params=_params(("arbitrary",)),
    )(h, gain.reshape(1, d), wr)


def _row_copy(src_hbm, idx, buf, slot, r, sem):
    src = src_hbm.at[idx] if len(src_hbm.shape) == 3 else src_hbm.at[pl.ds(idx, 1), :]
    return pltpu.make_async_copy(src, buf.at[slot, pl.ds(r, 1), :], sem.at[slot])


def _gather_start(src_hbm, idx_ref, base, buf, slot, sem, n_rows):
    def body(r, c):
        _row_copy(src_hbm, idx_ref[base + r], buf, slot, r, sem).start()
        return c
    lax.fori_loop(0, n_rows, body, 0, unroll=8)


def _gather_wait(src_hbm, buf, slot, sem, n_rows):
    def body(r, c):
        _row_copy(src_hbm, 0, buf, slot, r, sem).wait()
        return c
    lax.fori_loop(0, n_rows, body, 0, unroll=8)


def _expert_changed(te_ref, t):
    return (t == 0) | (te_ref[t] != te_ref[jnp.maximum(t - 1, 0)])


def _moe_up_kernel(te_ref, nv_ref, tok_ref, nxt_ref, h_hbm, g_ref, w1_hbm, w3_hbm, o_ref,
                   buf, sem, w1f_ref, w3f_ref, wsem, w1b_ref, w3b_ref, *, tm, e0):
    t = pl.program_id(0)
    n_valid = nv_ref[0]

    def weight_copies(e):
        return (pltpu.make_async_copy(w1_hbm.at[e0 + e], w1f_ref, wsem.at[0]),
                pltpu.make_async_copy(w3_hbm.at[e0 + e], w3f_ref, wsem.at[1]))

    @pl.when(t == 0)
    def _():
        for cp in weight_copies(te_ref[0]):
            cp.start()
        _gather_start(h_hbm, tok_ref, 0, buf, 0, sem, tm)

    @pl.when(_expert_changed(te_ref, t))
    def _():
        for cp in weight_copies(te_ref[t]):
            cp.wait()
        w1b_ref[...] = w1f_ref[...].astype(BF16)
        w3b_ref[...] = w3f_ref[...].astype(BF16)

        @pl.when(nxt_ref[t] >= 0)
        def _():
            for cp in weight_copies(nxt_ref[t]):
                cp.start()

    def step(slot):
        _gather_wait(h_hbm, buf, slot, sem, tm)
        base = jnp.minimum(t + 1, n_valid - 1) * tm
        for r in range(tm):
            _row_copy(h_hbm, tok_ref[base + r], buf, 1 - slot, r, sem).start()
        x = buf[slot]
        hn = (x * lax.rsqrt(jnp.mean(x * x, axis=-1, keepdims=True) + EPS) * g_ref[...]).astype(BF16)
        a = jnp.dot(hn, w1b_ref[...], preferred_element_type=F32)
        b = jnp.dot(hn, w3b_ref[...], preferred_element_type=F32)
        o_ref[...] = (a * jax.nn.sigmoid(a) * b).astype(o_ref.dtype)

        @pl.when(t == n_valid - 1)
        def _():
            _gather_wait(h_hbm, buf, 1 - slot, sem, tm)

    for slot in range(2):
        @pl.when((t < n_valid) & (t % 2 == slot))
        def _():
            step(slot)

    @pl.when(t >= n_valid)
    def _():
        o_ref[...] = jnp.zeros(o_ref.shape, o_ref.dtype)


def _moe_down_kernel(te_ref, nv_ref, h_ref, w2_ref, o_ref, w2b_ref):
    t = pl.program_id(0)

    @pl.when(_expert_changed(te_ref, t))
    def _():
        w2b_ref[...] = w2_ref[0].astype(BF16)

    @pl.when(t < nv_ref[0])
    def _():
        o_ref[...] = jnp.dot(h_ref[...], w2b_ref[...], preferred_element_type=F32)

    @pl.when(t >= nv_ref[0])
    def _():
        o_ref[...] = jnp.zeros(o_ref.shape, o_ref.dtype)


def _combine_kernel(pos_ref, y_hbm, h_ref, route_ref, *rest, tm, n_tok, n_norms):
    g_ref = rest[0] if n_norms else None
    o_ref = rest[1 if n_norms else 0]
    norm_refs = rest[2:2 + n_norms] if n_norms else ()
    buf0, buf1, sem0, sem1 = rest[-4:]
    t = pl.program_id(0)
    n = pl.num_programs(0)
    slot = t % 2

    def start(step, s):
        _gather_start(y_hbm, pos_ref, step * tm, buf0, s, sem0, tm)
        _gather_start(y_hbm, pos_ref, n_tok + step * tm, buf1, s, sem1, tm)

    @pl.when(t == 0)
    def _():
        start(0, 0)

    @pl.when(t + 1 < n)
    def _():
        start(t + 1, 1 - slot)

    _gather_wait(y_hbm, buf0, slot, sem0, tm)
    _gather_wait(y_hbm, buf1, slot, sem1, tm)
    w = route_ref[...]
    y = h_ref[...] + w[:, 2:3] * buf0[slot] + w[:, 3:4] * buf1[slot]
    o_ref[...] = y
    if n_norms:
        yn = y * lax.rsqrt(jnp.mean(y * y, axis=-1, keepdims=True) + EPS)
        for k, n_ref in enumerate(norm_refs):
            n_ref[...] = (yn * g_ref[k:k + 1, :]).astype(n_ref.dtype)


def _hier_moe(h, h_rows, gain, w_group, w_router, w1, w3, w2, layer, next_gains=None):
    t, d = h.shape
    f = w1.shape[2]
    ne = w_router.shape[0] * w_router.shape[2]
    e0 = layer * ne
    tm, tc = MOE_TM, COMBINE_TM
    route, counts = _router(h, gain, w_group, w_router)
    e_pair = jnp.concatenate([route[:, 0], route[:, 1]]).astype(jnp.int32)
    rank = jnp.concatenate([route[:, 4], route[:, 5]]).astype(jnp.int32)

    counts = counts[0, :ne].astype(jnp.int32)
    tiles_e = (counts + tm - 1) // tm
    tile_end = jnp.cumsum(tiles_e)
    row_start = (tile_end - tiles_e) * tm
    is_e = e_pair[:, None] == jnp.arange(ne, dtype=jnp.int32)[None, :]
    dest = jnp.sum(jnp.where(is_e, row_start[None, :], 0), axis=1) + rank
    n_tiles = (TOP_K * t) // tm + ne
    n_rows = n_tiles * tm
    dest = dest.astype(jnp.int32)
    tok_pair = jnp.tile(jnp.arange(t, dtype=jnp.int32), TOP_K)
    tok_sorted = jnp.zeros((n_rows,), jnp.int32).at[dest].set(tok_pair)
    n_valid = tile_end[-1]
    tile_ids = jnp.minimum(jnp.arange(n_tiles, dtype=jnp.int32), n_valid - 1)
    tile_expert = jnp.sum((tile_end[None, :] <= tile_ids[:, None]).astype(jnp.int32), axis=1)
    run_end = jnp.sum(jnp.where(tile_expert[:, None] == jnp.arange(ne, dtype=jnp.int32)[None, :],
                                tile_end[None, :], 0), axis=1)
    follows = run_end[:, None] == jnp.arange(n_tiles, dtype=jnp.int32)[None, :]
    next_expert = jnp.where(run_end < n_valid, jnp.sum(jnp.where(follows, tile_expert[None, :], 0), axis=1), -1)
    n_valid = n_valid.reshape(1).astype(jnp.int32)

    hid = pl.pallas_call(
        functools.partial(_moe_up_kernel, tm=tm, e0=e0),
        grid_spec=pltpu.PrefetchScalarGridSpec(
            num_scalar_prefetch=4,
            grid=(n_tiles,),
            in_specs=[pl.BlockSpec(memory_space=pl.ANY),
                      pl.BlockSpec((1, d), lambda i, te, nv, tok, nxt: (0, 0)),
                      pl.BlockSpec(memory_space=pl.ANY),
                      pl.BlockSpec(memory_space=pl.ANY)],
            out_specs=pl.BlockSpec((tm, f), lambda i, te, nv, tok, nxt: (i, 0)),
            scratch_shapes=[pltpu.VMEM((2, tm, d), F32), pltpu.SemaphoreType.DMA((2,)),
                            pltpu.VMEM((d, f), F32), pltpu.VMEM((d, f), F32), pltpu.SemaphoreType.DMA((2,)),
                            pltpu.VMEM((d, f), BF16), pltpu.VMEM((d, f), BF16)]),
        out_shape=jax.ShapeDtypeStruct((n_rows, f), BF16),
        name="moe_up",
        compiler_params=_params(("arbitrary",)),
    )(tile_expert, n_valid, tok_sorted, next_expert.astype(jnp.int32), h_rows, gain.reshape(1, d), w1, w3)

    ys = pl.pallas_call(
        _moe_down_kernel,
        grid_spec=pltpu.PrefetchScalarGridSpec(
            num_scalar_prefetch=2,
            grid=(n_tiles,),
            in_specs=[pl.BlockSpec((tm, f), lambda i, te, nv: (i, 0)),
                      pl.BlockSpec((1, f, d), lambda i, te, nv: (e0 + te[i], 0, 0))],
            out_specs=pl.BlockSpec((tm, d), lambda i, te, nv: (i, 0)),
            scratch_shapes=[pltpu.VMEM((f, d), BF16)]),
        out_shape=jax.ShapeDtypeStruct((n_rows, d), F32),
        name="moe_down",
        compiler_params=_params(("arbitrary",)),
    )(tile_expert, n_valid, hid, w2)

    n_norms = 0 if next_gains is None else next_gains.shape[0]
    row_spec = pl.BlockSpec((tc, d), lambda i, pos: (i, 0))
    in_specs = [pl.BlockSpec(memory_space=pl.ANY), row_spec, pl.BlockSpec((tc, LANES), lambda i, pos: (i, 0))]
    args = [dest.astype(jnp.int32), ys, h, route]
    if n_norms:
        in_specs.append(pl.BlockSpec((n_norms, d), lambda i, pos: (0, 0)))
        args.append(next_gains)
    outs = pl.pallas_call(
        functools.partial(_combine_kernel, tm=tc, n_tok=t, n_norms=n_norms),
        grid_spec=pltpu.PrefetchScalarGridSpec(
            num_scalar_prefetch=1,
            grid=(t // tc,),
            in_specs=in_specs,
            out_specs=[row_spec] * (1 + n_norms),
            scratch_shapes=[pltpu.VMEM((2, tc, d), F32), pltpu.VMEM((2, tc, d), F32),
                            pltpu.SemaphoreType.DMA((2,)), pltpu.SemaphoreType.DMA((2,))]),
        out_shape=[jax.ShapeDtypeStruct((t, d), F32)] + [jax.ShapeDtypeStruct((t, d), BF16)] * n_norms,
        name="moe_combine",
        compiler_params=_params(("arbitrary",)),
    )(*args)
    return outs[0], tuple(outs[1:])


def kernel(x, a_norm, a_w_in, a_b_in, a_ln_g, a_ln_b, a_w_s, a_b_s, a_w_out, kv_norm, w_k, w_v, k_norm, b_norm, w_q, q_norm, lam_q1, lam_k1, lam_q2, lam_k2, subln, w_o, rel_bias, m_norm, m_w_group, m_w_router, m_w1, m_w3, m_w2):
    batch, seq, d = x.shape
    depth = m_norm.shape[0]
    n_a = a_norm.shape[0]
    h = x.reshape(batch * seq, d)
    f = m_w1.shape[-1]
    w1 = m_w1.reshape(-1, d, f)
    w3 = m_w3.reshape(-1, d, f)
    w2 = m_w2.reshape(-1, f, d)
    def pre_norm_gains(layer):
        if layer >= depth:
            return None
        if layer < n_a:
            return a_norm[layer:layer + 1]
        j = layer - n_a
        return jnp.stack([kv_norm, b_norm[j]]) if layer == n_a else b_norm[j:j + 1]

    k = v = None
    normed = _rmsnorm(h, pre_norm_gains(0))
    for layer in range(depth):
        if layer < n_a:
            i = layer
            (hn,) = normed
            z = _matmul(hn, a_w_in, i, epilogue="gelu_bias", extra=a_b_in[i])
            gz = _gmlp_gate(z, a_ln_g[i], a_ln_b[i], a_w_s[i], a_b_s[i])
            h, h_rows = _matmul(gz, a_w_out, i, epilogue="residual", extra=h, out_dtype=F32)
        else:
            j = layer - n_a
            if layer == n_a:
                hkv, hq = normed
                k = _matmul(hkv, w_k, epilogue="headnorm", extra=k_norm)
                v = _matmul(hkv, w_v)
            else:
                (hq,) = normed
            q = _matmul(hq, w_q, j, epilogue="headnorm", extra=q_norm[j], scale=HEAD_DIM ** -0.5)
            lam_vecs = jnp.stack([lam_q1[j], lam_k1[j], lam_q2[j], lam_k2[j]]).astype(F32)
            o = _diff_attention(q, k, v, rel_bias, lam_vecs, subln[j], batch=batch, seq=seq,
                                lambda_init=_lambda_init(layer))
            h, h_rows = _matmul(o, w_o, j, epilogue="residual", extra=h, out_dtype=F32)
        h, normed = _hier_moe(h, h_rows, m_norm[layer], m_w_group[layer], m_w_router[layer], w1, w3, w2, layer,
                              next_gains=pre_norm_gains(layer + 1))
    return h.reshape(batch, seq, d)
```

```python
import functools
import math

import jax
import jax.numpy as jnp
import numpy as np
from jax import lax
from jax.experimental import pallas as pl
from jax.experimental.pallas import tpu as pltpu

F32 = jnp.float32
BF16 = jnp.bfloat16

CHUNK = 128
HEAD_DIM = 128
N_BUCKETS = 32
MAX_DISTANCE = 128
FAR_DISTANCE = MAX_DISTANCE
TOP_K = 2
EPS = 1e-6
LN_EPS = 1e-5
NEG_INF = -1e30
FINITE_MAX = 3.0e38

LANES = 128
VMEM_LIMIT_BYTES = 56 * 1024 * 1024

NORM_TM = 512
MM_TM, MM_TM_NARROW_OUT, MM_TN = 512, 1024, 1024
GATE_TM = 512
ATTN_TQ, ATTN_TK = 2048, 512
ROUTER_TM = 512
MOE_TM = 256
COMBINE_TM = 256


def _params(semantics):
    return pltpu.CompilerParams(dimension_semantics=semantics, vmem_limit_bytes=VMEM_LIMIT_BYTES)


def _lambda_init(layer):
    return 0.8 - 0.6 * math.exp(-0.3 * layer)


def _rmsnorm_kernel(x_ref, g_ref, *o_refs):
    x = x_ref[...]
    y = x * lax.rsqrt(jnp.mean(x * x, axis=-1, keepdims=True) + EPS)
    for k, o_ref in enumerate(o_refs):
        o_ref[...] = (y * g_ref[k:k + 1, :]).astype(o_ref.dtype)


def _rmsnorm(x, gains):
    t, d = x.shape
    n = gains.shape[0]
    tm = min(NORM_TM, t)
    return pl.pallas_call(
        _rmsnorm_kernel,
        grid=(t // tm,),
        in_specs=[pl.BlockSpec((tm, d), lambda i: (i, 0)),
                  pl.BlockSpec((n, d), lambda i: (0, 0))],
        out_specs=[pl.BlockSpec((tm, d), lambda i: (i, 0))] * n,
        out_shape=[jax.ShapeDtypeStruct((t, d), BF16)] * n,
        name="rmsnorm",
        compiler_params=_params(("parallel",)),
    )(x, gains)


def _gelu_tanh(x):
    return 0.5 * x * (1.0 + jnp.tanh(math.sqrt(2.0 / math.pi) * (x + 0.044715 * (x * x * x))))


def _mm_kernel(a_ref, w_ref, *rest, epilogue, scale):
    wb_ref = rest[-1]
    o_ref = rest[1] if epilogue == "residual" else rest[-2]

    @pl.when(pl.program_id(1) == 0)
    def _():
        wb_ref[...] = w_ref[...].astype(BF16)

    acc = jnp.dot(a_ref[...], wb_ref[...], preferred_element_type=F32)
    if epilogue == "gelu_bias":
        o_ref[...] = _gelu_tanh(acc + rest[0][...]).astype(o_ref.dtype)
    elif epilogue == "residual":
        y = rest[0][...] + acc
        o_ref[...] = y
        rest[2][...] = y
    elif epilogue == "headnorm":
        gain = rest[0][...] * scale
        for g in range(acc.shape[1] // HEAD_DIM):
            y = acc[:, g * HEAD_DIM:(g + 1) * HEAD_DIM]
            y = y * lax.rsqrt(jnp.mean(y * y, axis=-1, keepdims=True) + EPS)
            o_ref[:, g * HEAD_DIM:(g + 1) * HEAD_DIM] = (y * gain).astype(o_ref.dtype)
    else:
        o_ref[...] = acc.astype(o_ref.dtype)


def _matmul(a, w, layer=0, *, epilogue="none", extra=None, scale=1.0, out_dtype=BF16):
    m, k = a.shape
    if w.ndim == 2:
        w = w.reshape(1, *w.shape)
    n = w.shape[2]
    tm, tn = min(MM_TM if out_dtype == F32 else MM_TM_NARROW_OUT, m), min(MM_TN, n)
    in_specs = [pl.BlockSpec((tm, k), lambda j, i: (i, 0)),
                pl.BlockSpec((None, k, tn), lambda j, i: (layer, 0, j), pipeline_mode=pl.Buffered(1))]
    args = [a, w]
    if epilogue == "gelu_bias":
        in_specs.append(pl.BlockSpec((1, tn), lambda j, i: (0, j)))
        args.append(extra.reshape(1, n))
    elif epilogue == "residual":
        in_specs.append(pl.BlockSpec((tm, tn), lambda j, i: (i, j)))
        args.append(extra)
    elif epilogue == "headnorm":
        in_specs.append(pl.BlockSpec((1, HEAD_DIM), lambda j, i: (0, 0)))
        args.append(extra.reshape(1, HEAD_DIM))
    out_specs = pl.BlockSpec((tm, tn), lambda j, i: (i, j))
    out_shape = jax.ShapeDtypeStruct((m, n), out_dtype)
    if epilogue == "residual":
        out_specs = [out_specs, pl.BlockSpec((tm, None, tn), lambda j, i: (i, 0, j))]
        out_shape = [out_shape, jax.ShapeDtypeStruct((m, 1, n), out_dtype)]
    return pl.pallas_call(
        functools.partial(_mm_kernel, epilogue=epilogue, scale=scale),
        grid=(n // tn, m // tm),
        in_specs=in_specs,
        out_specs=out_specs,
        out_shape=out_shape,
        scratch_shapes=[pltpu.VMEM((k, tn), BF16)],
        name="mm_" + epilogue,
        compiler_params=_params(("parallel", "arbitrary")),
    )(*args)


def _gmlp_gate_kernel(u_ref, v_ref, lng_ref, lnb_ref, ws_ref, bs_ref, o_ref, *, n_chunks, groups):
    v = v_ref[...].astype(F32)
    mu = jnp.mean(v, axis=-1, keepdims=True)
    vc = v - mu
    var = jnp.mean(vc * vc, axis=-1, keepdims=True)
    vn = (vc * lax.rsqrt(var + LN_EPS) * lng_ref[...] + lnb_ref[...]).astype(BF16)
    gw = v.shape[1] // groups
    row = lax.broadcasted_iota(jnp.int32, (CHUNK, CHUNK), 0)
    col = lax.broadcasted_iota(jnp.int32, (CHUNK, CHUNK), 1)
    causal = row >= col
    for g in range(groups):
        w = jnp.where(causal, ws_ref[g], 0.0).astype(BF16)
        b = bs_ref[g]
        for c in range(n_chunks):
            rows = slice(c * CHUNK, (c + 1) * CHUNK)
            cols = slice(g * gw, (g + 1) * gw)
            s = jnp.dot(w, vn[rows, cols], preferred_element_type=F32) + b
            o_ref[rows, cols] = (u_ref[rows, cols].astype(F32) * s).astype(o_ref.dtype)


def _gmlp_gate(z, ln_g, ln_b, w_s, b_s):
    t, d2 = z.shape
    dg = d2 // 2
    groups = w_s.shape[0]
    tm = min(GATE_TM, t)
    return pl.pallas_call(
        functools.partial(_gmlp_gate_kernel, n_chunks=tm // CHUNK, groups=groups),
        grid=(t // tm,),
        in_specs=[pl.BlockSpec((tm, dg), lambda i: (i, 0)),
                  pl.BlockSpec((tm, dg), lambda i: (i, 1)),
                  pl.BlockSpec((1, dg), lambda i: (0, 0)),
                  pl.BlockSpec((1, dg), lambda i: (0, 0)),
                  pl.BlockSpec((groups, CHUNK, CHUNK), lambda i: (0, 0, 0)),
                  pl.BlockSpec((groups, CHUNK, 1), lambda i: (0, 0, 0))],
        out_specs=pl.BlockSpec((tm, dg), lambda i: (i, 0)),
        out_shape=jax.ShapeDtypeStruct((t, dg), BF16),
        name="gmlp_gate",
        compiler_params=_params(("parallel",)),
    )(z, z, ln_g.reshape(1, dg), ln_b.reshape(1, dg), w_s, b_s.reshape(groups, CHUNK, 1))


def _t5_causal_bucket(rel):
    n = jnp.maximum(rel, 0)
    max_exact = N_BUCKETS // 2
    nf = jnp.maximum(n, 1).astype(F32)
    large = max_exact + (jnp.log(nf / max_exact) / math.log(MAX_DISTANCE / max_exact)
                         * (N_BUCKETS - max_exact)).astype(jnp.int32)
    large = jnp.minimum(large, N_BUCKETS - 1)
    return jnp.where(n < max_exact, n, large)


def _attn_kernel(lam_ref, qt_ref, k_ref, vt_ref, z_ref, subln_ref, o_ref, acc_ref, m_ref, l_ref, bias_ref,
                 *, tq, tk, lambda_init):
    qi = pl.program_id(2)
    ratio = tq // tk
    first_near = ratio * qi - 1

    @pl.when(qi == 0)
    def _():
        bias_ref[0] = pltpu.roll(jnp.broadcast_to(z_ref[0], (tk, 2 * tq)), 0, 1, stride=1, stride_axis=0)

    def first_query(near):
        return 0 if near is None else max(0, -near * tk)

    def row_pad(x, c0, fill):
        return x if c0 == 0 else jnp.concatenate([jnp.full((1, c0), fill, F32), x], axis=1)

    def scores(j, c, near):
        ks = pl.multiple_of(j * tk, tk)
        c0 = first_query(near)
        k = k_ref[0, pl.ds(ks, tk), c * HEAD_DIM:(c + 1) * HEAD_DIM]
        qt = qt_ref[0, c * HEAD_DIM:(c + 1) * HEAD_DIM, c0:]
        s = jnp.dot(k, qt, preferred_element_type=F32)
        if near is not None:
            d = near * tk
            lo = d + tq - tk + c0
            nb = min(tq - c0, (tk if d <= 0 else 0) + FAR_DISTANCE)
            sn = s[:, :nb] + bias_ref[0, :, lo:lo + nb]
            if d <= 0:
                valid = (lax.broadcasted_iota(jnp.int32, sn.shape, 1) + (c0 + d)
                         >= lax.broadcasted_iota(jnp.int32, sn.shape, 0))
                sn = jnp.where(valid, sn, NEG_INF)
            s = sn if nb == tq - c0 else jnp.concatenate([sn, s[:, nb:]], axis=1)
        return s

    def fixed_step(j, near):
        vt = vt_ref[0, 0, j]
        c0 = first_query(near)
        for c in range(2):
            p = jnp.exp(scores(j, c, near) - m_ref[c][:, c0:])
            l_ref[c] += row_pad(jnp.sum(p, axis=0, keepdims=True), c0, 0.0)
            acc_ref[c, :, c0:] += jnp.dot(vt, p.astype(BF16), preferred_element_type=F32)

    def online_step(j, near):
        vt = vt_ref[0, 0, j]
        c0 = first_query(near)
        for c in range(2):
            s = scores(j, c, near)
            m_old = m_ref[c]
            m_prev = m_old[:, c0:]
            m_new = jnp.maximum(m_prev, jnp.max(s, axis=0, keepdims=True))
            alpha = jnp.exp(m_prev - m_new)
            p = jnp.exp(s - m_new)
            l_ref[c] = row_pad(alpha, c0, 1.0) * l_ref[c] + row_pad(jnp.sum(p, axis=0, keepdims=True), c0, 0.0)
            acc_ref[c, :, c0:] = (alpha * acc_ref[c, :, c0:]
                                  + jnp.dot(vt, p.astype(BF16), preferred_element_type=F32))
            m_ref[c] = m_new if c0 == 0 else jnp.concatenate([m_old[:, :c0], m_new], axis=1)

    def all_blocks(step):
        def far_body(j, carry):
            step(j, None)
            return carry
        lax.fori_loop(0, jnp.maximum(first_near, 0), far_body, 0)

        @pl.when(qi >= 1)
        def _():
            step(first_near, 1)

        for r in range(ratio):
            step(first_near + 1 + r, -r)

    l_ref[...] = jnp.zeros(l_ref.shape, F32)
    acc_ref[...] = jnp.zeros(acc_ref.shape, F32)
    self_bias = bias_ref[0, 0:1, tq - tk:tq - tk + 1]
    q0 = pl.multiple_of(qi * tq, tq)
    for c in range(2):
        k_self = k_ref[0, pl.ds(q0, tq), c * HEAD_DIM:(c + 1) * HEAD_DIM].astype(F32)
        qt = qt_ref[0, c * HEAD_DIM:(c + 1) * HEAD_DIM, :].astype(F32)
        m_ref[c] = jnp.sum(qt * k_self.T, axis=0, keepdims=True) + self_bias
    all_blocks(fixed_step)

    n_bad = (jnp.sum(jnp.where(jnp.abs(acc_ref[...]) < FINITE_MAX, 0.0, 1.0))
             + jnp.sum(jnp.where(jnp.abs(l_ref[...]) < FINITE_MAX, 0.0, 1.0)))

    @pl.when(n_bad > 0.0)
    def _():
        m_ref[...] = jnp.full(m_ref.shape, NEG_INF, F32)
        l_ref[...] = jnp.zeros(l_ref.shape, F32)
        acc_ref[...] = jnp.zeros(acc_ref.shape, F32)
        all_blocks(online_step)

    lam_v = lam_ref[...]
    lam = (jnp.exp(jnp.sum(lam_v[0:1] * lam_v[1:2], axis=-1, keepdims=True))
           - jnp.exp(jnp.sum(lam_v[2:3] * lam_v[3:4], axis=-1, keepdims=True)) + lambda_init)
    ot = acc_ref[0] / l_ref[0] - lam * (acc_ref[1] / l_ref[1])
    ot = ot * lax.rsqrt(jnp.mean(ot * ot, axis=0, keepdims=True) + EPS)
    ot = ot * subln_ref[...] * (1.0 - lambda_init)
    o_ref[...] = ot.T.astype(o_ref.dtype)


def _bias_by_distance(rel_bias, tq, tk):
    table = (rel_bias - rel_bias[N_BUCKETS - 1:N_BUCKETS, :]).astype(F32)
    rel = jnp.arange(2 * tq, dtype=jnp.int32) - (tq - tk)
    onehot = (_t5_causal_bucket(rel)[:, None] == jnp.arange(N_BUCKETS, dtype=jnp.int32)[None, :]).astype(F32)
    z = jnp.dot(onehot, table, precision=lax.Precision.HIGHEST).T
    return z.reshape(z.shape[0], 1, 2 * tq)


def _diff_attention(q, k, v, rel_bias, lam_vecs, subln, *, batch, seq, lambda_init):
    t, d = q.shape
    hd2 = 2 * HEAD_DIM
    n_heads = d // hd2
    tq = min(ATTN_TQ, seq)
    tk = min(ATTN_TK, tq)
    nq, nk = seq // tq, seq // tk
    assert tk >= FAR_DISTANCE and tq % tk == 0 and seq % tq == 0
    far = np.arange(FAR_DISTANCE + 1, max(seq, FAR_DISTANCE + 2), dtype=np.int64)
    far_bucket = N_BUCKETS // 2 + (np.log(far.astype(np.float32) / (N_BUCKETS // 2))
                                   / math.log(MAX_DISTANCE / (N_BUCKETS // 2))
                                   * (N_BUCKETS - N_BUCKETS // 2)).astype(np.int64)
    assert np.all(far_bucket >= N_BUCKETS - 1)
    z = _bias_by_distance(rel_bias, tq, tk)
    qt = jnp.transpose(q.reshape(batch, seq, d), (0, 2, 1))
    k3 = k.reshape(batch, seq, d)
    vt = jnp.transpose(v.reshape(batch, nk, tk, n_heads, hd2), (0, 3, 1, 4, 2))
    return pl.pallas_call(
        functools.partial(_attn_kernel, tq=tq, tk=tk, lambda_init=lambda_init),
        grid=(batch, n_heads, nq),
        in_specs=[pl.BlockSpec((4, HEAD_DIM), lambda b, h, i: (0, 0)),
                  pl.BlockSpec((1, hd2, tq), lambda b, h, i: (b, h, i)),
                  pl.BlockSpec((1, seq, hd2), lambda b, h, i: (b, 0, h)),
                  pl.BlockSpec((1, 1, nk, hd2, tk), lambda b, h, i: (b, h, 0, 0, 0)),
                  pl.BlockSpec((1, 1, 2 * tq), lambda b, h, i: (h, 0, 0)),
                  pl.BlockSpec((hd2, 1), lambda b, h, i: (0, 0))],
        out_specs=pl.BlockSpec((tq, hd2), lambda b, h, i: (b * nq + i, h)),
        out_shape=jax.ShapeDtypeStruct((t, d), BF16),
        scratch_shapes=[pltpu.VMEM((2, hd2, tq), F32),
                        pltpu.VMEM((2, 1, tq), F32),
                        pltpu.VMEM((2, 1, tq), F32),
                        pltpu.VMEM((1, tk, 2 * tq), F32)],
        name="diff_attn",
        compiler_params=_params(("parallel", "parallel", "arbitrary")),
    )(lam_vecs, qt, k3, vt, z, subln.reshape(hd2, 1))


def _router_kernel(x_ref, g_ref, wr_ref, o_ref, cnt_ref, *, n_groups, n_experts):
    @pl.when(pl.program_id(0) == 0)
    def _():
        cnt_ref[...] = jnp.zeros(cnt_ref.shape, F32)

    x = x_ref[...]
    hn = x * lax.rsqrt(jnp.mean(x * x, axis=-1, keepdims=True) + EPS) * g_ref[...]
    hi = hn.astype(BF16)
    lo = (hn - hi.astype(F32)).astype(BF16)
    r1 = jnp.dot(hi, wr_ref[...], preferred_element_type=F32)
    r2 = jnp.dot(lo, wr_ref[:, :LANES], preferred_element_type=F32)
    logits = r1[:, :LANES] + r1[:, LANES:] + r2
    lane = lax.broadcasted_iota(jnp.int32, logits.shape, 1).astype(F32)
    no_lane = float(LANES)
    is_g = lane < n_groups
    gl = jnp.where(is_g, logits, -jnp.inf)
    gmax = jnp.max(gl, axis=-1, keepdims=True)
    gidx = jnp.min(jnp.where(gl == gmax, lane, no_lane), axis=-1, keepdims=True)
    gsum = jnp.sum(jnp.where(is_g, jnp.exp(gl - gmax), 0.0), axis=-1, keepdims=True)
    g_gate = 1.0 / gsum
    lo = n_groups + gidx * n_experts
    in_grp = (lane >= lo) & (lane < lo + n_experts)
    el = jnp.where(in_grp, logits, -jnp.inf)
    t1 = jnp.max(el, axis=-1, keepdims=True)
    i1 = jnp.min(jnp.where(el == t1, lane, no_lane), axis=-1, keepdims=True)
    el2 = jnp.where(lane == i1, -jnp.inf, el)
    t2 = jnp.max(el2, axis=-1, keepdims=True)
    i2 = jnp.min(jnp.where(el2 == t2, lane, no_lane), axis=-1, keepdims=True)
    e = jnp.exp(t2 - t1)
    w1 = g_gate / (1.0 + e)
    w2 = g_gate * e / (1.0 + e)
    e1 = i1 - n_groups
    e2 = i2 - n_groups
    tm = logits.shape[0]
    oh1 = lane == e1
    oh2 = lane == e2
    strict_lower = (lax.broadcasted_iota(jnp.int32, (tm, tm), 0)
                    > lax.broadcasted_iota(jnp.int32, (tm, tm), 1)).astype(BF16)
    pre1 = jnp.dot(strict_lower, oh1.astype(BF16), preferred_element_type=F32)
    pre2 = jnp.dot(strict_lower, oh2.astype(BF16), preferred_element_type=F32)
    tot1 = jnp.sum(oh1.astype(F32), axis=0, keepdims=True)
    tot2 = jnp.sum(oh2.astype(F32), axis=0, keepdims=True)
    base = cnt_ref[...]
    rank1 = jnp.sum(jnp.where(oh1, base + pre1, 0.0), axis=-1, keepdims=True)
    rank2 = jnp.sum(jnp.where(oh2, base + tot1 + pre2, 0.0), axis=-1, keepdims=True)
    cnt_ref[...] = base + tot1 + tot2
    out = jnp.zeros(logits.shape, F32)
    for k, val in enumerate((e1, e2, w1, w2, rank1, rank2)):
        out = jnp.where(lane == k, val, out)
    o_ref[...] = out


def _router(h, gain, w_group, w_router):
    t, d = h.shape
    n_groups, _, n_experts = w_router.shape
    tm = min(ROUTER_TM, t)
    assert n_groups + n_groups * n_experts <= LANES
    wr = jnp.concatenate([w_group, jnp.transpose(w_router, (1, 0, 2)).reshape(d, n_groups * n_experts)], axis=1)
    wr = jnp.pad(wr, ((0, 0), (0, LANES - wr.shape[1])))
    wr_hi = wr.astype(BF16)
    wr = jnp.concatenate([wr_hi, (wr - wr_hi.astype(F32)).astype(BF16)], axis=1)
    return pl.pallas_call(
        functools.partial(_router_kernel, n_groups=n_groups, n_experts=n_experts),
        grid=(t // tm,),
        in_specs=[pl.BlockSpec((tm, d), lambda i: (i, 0)),
                  pl.BlockSpec((1, d), lambda i: (0, 0)),
                  pl.BlockSpec((d, 2 * LANES), lambda i: (0, 0))],
        out_specs=[pl.BlockSpec((tm, LANES), lambda i: (i, 0)),
                   pl.BlockSpec((1, LANES), lambda i: (0, 0))],
        out_shape=[jax.ShapeDtypeStruct((t, LANES), F32),
                   jax.ShapeDtypeStruct((1, LANES), F32)],
        name="moe_router",
        compiler_params=_params(("arbitrary",)),
    )(h, gain.reshape(1, d), wr)


def _row_copy(src_hbm, idx, buf, slot, r, sem):
    src = src_hbm.at[idx] if len(src_hbm.shape) == 3 else src_hbm.at[pl.ds(idx, 1), :]
    return pltpu.make_async_copy(src, buf.at[slot, pl.ds(r, 1), :], sem.at[slot])


def _gather_start(src_hbm, idx_ref, base, buf, slot, sem, n_rows):
    def body(r, c):
        _row_copy(src_hbm, idx_ref[base + r], buf, slot, r, sem).start()
        return c
    lax.fori_loop(0, n_rows, body, 0, unroll=8)


def _gather_wait(src_hbm, buf, slot, sem, n_rows):
    def body(r, c):
        _row_copy(src_hbm, 0, buf, slot, r, sem).wait()
        return c
    lax.fori_loop(0, n_rows, body, 0, unroll=8)


def _expert_changed(te_ref, t):
    return (t == 0) | (te_ref[t] != te_ref[jnp.maximum(t - 1, 0)])


def _moe_up_kernel(te_ref, nv_ref, tok_ref, nxt_ref, h_hbm, g_ref, w1_hbm, w3_hbm, o_ref,
                   buf, sem, w1f_ref, w3f_ref, wsem, w1b_ref, w3b_ref, *, tm, e0):
    t = pl.program_id(0)
    n_valid = nv_ref[0]

    def weight_copies(e):
        return (pltpu.make_async_copy(w1_hbm.at[e0 + e], w1f_ref, wsem.at[0]),
                pltpu.make_async_copy(w3_hbm.at[e0 + e], w3f_ref, wsem.at[1]))

    @pl.when(t == 0)
    def _():
        for cp in weight_copies(te_ref[0]):
            cp.start(priority=1)
        _gather_start(h_hbm, tok_ref, 0, buf, 0, sem, tm)

    @pl.when(_expert_changed(te_ref, t))
    def _():
        for cp in weight_copies(te_ref[t]):
            cp.wait()
        w1b_ref[...] = w1f_ref[...].astype(BF16)
        w3b_ref[...] = w3f_ref[...].astype(BF16)

        @pl.when(nxt_ref[t] >= 0)
        def _():
            for cp in weight_copies(nxt_ref[t]):
                cp.start(priority=1)

    def step(slot):
        _gather_wait(h_hbm, buf, slot, sem, tm)
        base = jnp.minimum(t + 1, n_valid - 1) * tm
        for r in range(tm):
            _row_copy(h_hbm, tok_ref[base + r], buf, 1 - slot, r, sem).start()
        x = buf[slot]
        hn = (x * lax.rsqrt(jnp.mean(x * x, axis=-1, keepdims=True) + EPS) * g_ref[...]).astype(BF16)
        a = jnp.dot(hn, w1b_ref[...], preferred_element_type=F32)
        b = jnp.dot(hn, w3b_ref[...], preferred_element_type=F32)
        o_ref[...] = (a * jax.nn.sigmoid(a) * b).astype(o_ref.dtype)

        @pl.when(t == n_valid - 1)
        def _():
            _gather_wait(h_hbm, buf, 1 - slot, sem, tm)

    for slot in range(2):
        @pl.when((t < n_valid) & (t % 2 == slot))
        def _():
            step(slot)

    @pl.when(t >= n_valid)
    def _():
        o_ref[...] = jnp.zeros(o_ref.shape, o_ref.dtype)


def _moe_down_kernel(te_ref, nv_ref, h_ref, w2_ref, o_ref, w2b_ref):
    t = pl.program_id(0)

    @pl.when(_expert_changed(te_ref, t))
    def _():
        w2b_ref[...] = w2_ref[0].astype(BF16)

    @pl.when(t < nv_ref[0])
    def _():
        o_ref[...] = jnp.dot(h_ref[...], w2b_ref[...], preferred_element_type=F32)

    @pl.when(t >= nv_ref[0])
    def _():
        o_ref[...] = jnp.zeros(o_ref.shape, o_ref.dtype)


def _combine_kernel(pos_ref, y_hbm, h_ref, route_ref, *rest, tm, n_tok, n_norms):
    g_ref = rest[0] if n_norms else None
    o_ref = rest[1 if n_norms else 0]
    norm_refs = rest[2:2 + n_norms] if n_norms else ()
    buf0, buf1, sem0, sem1 = rest[-4:]
    t = pl.program_id(0)
    n = pl.num_programs(0)
    slot = t % 2

    def start(step, s):
        _gather_start(y_hbm, pos_ref, step * tm, buf0, s, sem0, tm)
        _gather_start(y_hbm, pos_ref, n_tok + step * tm, buf1, s, sem1, tm)

    @pl.when(t == 0)
    def _():
        start(0, 0)

    @pl.when(t + 1 < n)
    def _():
        start(t + 1, 1 - slot)

    _gather_wait(y_hbm, buf0, slot, sem0, tm)
    _gather_wait(y_hbm, buf1, slot, sem1, tm)
    w = route_ref[...]
    y = h_ref[...] + w[:, 2:3] * buf0[slot] + w[:, 3:4] * buf1[slot]
    o_ref[...] = y
    if n_norms:
        yn = y * lax.rsqrt(jnp.mean(y * y, axis=-1, keepdims=True) + EPS)
        for k, n_ref in enumerate(norm_refs):
            n_ref[...] = (yn * g_ref[k:k + 1, :]).astype(n_ref.dtype)


def _hier_moe(h, h_rows, gain, w_group, w_router, w1, w3, w2, layer, next_gains=None):
    t, d = h.shape
    f = w1.shape[2]
    ne = w_router.shape[0] * w_router.shape[2]
    e0 = layer * ne
    tm, tc = MOE_TM, COMBINE_TM
    route, counts = _router(h, gain, w_group, w_router)
    e_pair = jnp.concatenate([route[:, 0], route[:, 1]]).astype(jnp.int32)
    rank = jnp.concatenate([route[:, 4], route[:, 5]]).astype(jnp.int32)

    counts = counts[0, :ne].astype(jnp.int32)
    tiles_e = (counts + tm - 1) // tm
    tile_end = jnp.cumsum(tiles_e)
    row_start = (tile_end - tiles_e) * tm
    is_e = e_pair[:, None] == jnp.arange(ne, dtype=jnp.int32)[None, :]
    dest = jnp.sum(jnp.where(is_e, row_start[None, :], 0), axis=1) + rank
    n_tiles = (TOP_K * t) // tm + ne
    n_rows = n_tiles * tm
    dest = dest.astype(jnp.int32)
    tok_pair = jnp.tile(jnp.arange(t, dtype=jnp.int32), TOP_K)
    tok_sorted = jnp.zeros((n_rows,), jnp.int32).at[dest].set(tok_pair)
    n_valid = tile_end[-1]
    tile_ids = jnp.minimum(jnp.arange(n_tiles, dtype=jnp.int32), n_valid - 1)
    tile_expert = jnp.sum((tile_end[None, :] <= tile_ids[:, None]).astype(jnp.int32), axis=1)
    run_end = jnp.sum(jnp.where(tile_expert[:, None] == jnp.arange(ne, dtype=jnp.int32)[None, :],
                                tile_end[None, :], 0), axis=1)
    follows = run_end[:, None] == jnp.arange(n_tiles, dtype=jnp.int32)[None, :]
    next_expert = jnp.where(run_end < n_valid, jnp.sum(jnp.where(follows, tile_expert[None, :], 0), axis=1), -1)
    n_valid = n_valid.reshape(1).astype(jnp.int32)

    hid = pl.pallas_call(
        functools.partial(_moe_up_kernel, tm=tm, e0=e0),
        grid_spec=pltpu.PrefetchScalarGridSpec(
            num_scalar_prefetch=4,
            grid=(n_tiles,),
            in_specs=[pl.BlockSpec(memory_space=pl.ANY),
                      pl.BlockSpec((1, d), lambda i, te, nv, tok, nxt: (0, 0)),
                      pl.BlockSpec(memory_space=pl.ANY),
                      pl.BlockSpec(memory_space=pl.ANY)],
            out_specs=pl.BlockSpec((tm, f), lambda i, te, nv, tok, nxt: (i, 0)),
            scratch_shapes=[pltpu.VMEM((2, tm, d), F32), pltpu.SemaphoreType.DMA((2,)),
                            pltpu.VMEM((d, f), F32), pltpu.VMEM((d, f), F32), pltpu.SemaphoreType.DMA((2,)),
                            pltpu.VMEM((d, f), BF16), pltpu.VMEM((d, f), BF16)]),
        out_shape=jax.ShapeDtypeStruct((n_rows, f), BF16),
        name="moe_up",
        compiler_params=_params(("arbitrary",)),
    )(tile_expert, n_valid, tok_sorted, next_expert.astype(jnp.int32), h_rows, gain.reshape(1, d), w1, w3)

    ys = pl.pallas_call(
        _moe_down_kernel,
        grid_spec=pltpu.PrefetchScalarGridSpec(
            num_scalar_prefetch=2,
            grid=(n_tiles,),
            in_specs=[pl.BlockSpec((tm, f), lambda i, te, nv: (i, 0)),
                      pl.BlockSpec((1, f, d), lambda i, te, nv: (e0 + te[i], 0, 0))],
            out_specs=pl.BlockSpec((tm, d), lambda i, te, nv: (i, 0)),
            scratch_shapes=[pltpu.VMEM((f, d), BF16)]),
        out_shape=jax.ShapeDtypeStruct((n_rows, d), F32),
        name="moe_down",
        compiler_params=_params(("arbitrary",)),
    )(tile_expert, n_valid, hid, w2)

    n_norms = 0 if next_gains is None else next_gains.shape[0]
    row_spec = pl.BlockSpec((tc, d), lambda i, pos: (i, 0))
    in_specs = [pl.BlockSpec(memory_space=pl.ANY), row_spec, pl.BlockSpec((tc, LANES), lambda i, pos: (i, 0))]
    args = [dest.astype(jnp.int32), ys, h, route]
    if n_norms:
        in_specs.append(pl.BlockSpec((n_norms, d), lambda i, pos: (0, 0)))
        args.append(next_gains)
    outs = pl.pallas_call(
        functools.partial(_combine_kernel, tm=tc, n_tok=t, n_norms=n_norms),
        grid_spec=pltpu.PrefetchScalarGridSpec(
            num_scalar_prefetch=1,
            grid=(t // tc,),
            in_specs=in_specs,
            out_specs=[row_spec] * (1 + n_norms),
            scratch_shapes=[pltpu.VMEM((2, tc, d), F32), pltpu.VMEM((2, tc, d), F32),
                            pltpu.SemaphoreType.DMA((2,)), pltpu.SemaphoreType.DMA((2,))]),
        out_shape=[jax.ShapeDtypeStruct((t, d), F32)] + [jax.ShapeDtypeStruct((t, d), BF16)] * n_norms,
        name="moe_combine",
        compiler_params=_params(("arbitrary",)),
    )(*args)
    return outs[0], tuple(outs[1:])


def kernel(x, a_norm, a_w_in, a_b_in, a_ln_g, a_ln_b, a_w_s, a_b_s, a_w_out, kv_norm, w_k, w_v, k_norm, b_norm, w_q, q_norm, lam_q1, lam_k1, lam_q2, lam_k2, subln, w_o, rel_bias, m_norm, m_w_group, m_w_router, m_w1, m_w3, m_w2):
    batch, seq, d = x.shape
    depth = m_norm.shape[0]
    n_a = a_norm.shape[0]
    h = x.reshape(batch * seq, d)
    f = m_w1.shape[-1]
    w1 = m_w1.reshape(-1, d, f)
    w3 = m_w3.reshape(-1, d, f)
    w2 = m_w2.reshape(-1, f, d)
    def pre_norm_gains(layer):
        if layer >= depth:
            return None
        if layer < n_a:
            return a_norm[layer:layer + 1]
        j = layer - n_a
        return jnp.stack([kv_norm, b_norm[j]]) if layer == n_a else b_norm[j:j + 1]

    k = v = None
    normed = _rmsnorm(h, pre_norm_gains(0))
    for layer in range(depth):
        if layer < n_a:
            i = layer
            (hn,) = normed
            z = _matmul(hn, a_w_in, i, epilogue="gelu_bias", extra=a_b_in[i])
            gz = _gmlp_gate(z, a_ln_g[i], a_ln_b[i], a_w_s[i], a_b_s[i])
            h, h_rows = _matmul(gz, a_w_out, i, epilogue="residual", extra=h, out_dtype=F32)
        else:
            j = layer - n_a
            if layer == n_a:
                hkv, hq = normed
                k = _matmul(hkv, w_k, epilogue="headnorm", extra=k_norm)
                v = _matmul(hkv, w_v)
            else:
                (hq,) = normed
            q = _matmul(hq, w_q, j, epilogue="headnorm", extra=q_norm[j], scale=HEAD_DIM ** -0.5)
            lam_vecs = jnp.stack([lam_q1[j], lam_k1[j], lam_q2[j], lam_k2[j]]).astype(F32)
            o = _diff_attention(q, k, v, rel_bias, lam_vecs, subln[j], batch=batch, seq=seq,
                                lambda_init=_lambda_init(layer))
            h, h_rows = _matmul(o, w_o, j, epilogue="residual", extra=h, out_dtype=F32)
        h, normed = _hier_moe(h, h_rows, m_norm[layer], m_w_group[layer], m_w_router[layer], w1, w3, w2, layer,
                              next_gains=pre_norm_gains(layer + 1))
    return h.reshape(batch, seq, d)
```
